```python
import math
import jax, jax.numpy as jnp
from jax import lax
import numpy as np

D_MODEL = 2048
BATCH = 2
SEQ = 4096
DEPTH = 2
DEC_BATCH = 32
DEC_SEQ = 8
PAST_LEN = 8192
PAGE_SIZE = 128

N_EVEN = (DEPTH + 1) // 2
N_ODD = DEPTH // 2
EPS = 1e-6
F32 = jnp.float32
A_WIDTH = D_MODEL // 2
A_HEADS = 8
A_BLOCK = A_WIDTH // A_HEADS
CONV_W = 4
LRU_C = 8.0
B_HEAD_DIM = 128
B_HEADS = (D_MODEL // 2) // B_HEAD_DIM
B_WIDTH = B_HEADS * B_HEAD_DIM
Q_BLOCK = 128
C_DK = 128
C_DV = 128
C_HEADS = (D_MODEL // 2) // C_DK
C_WIDTH = C_HEADS * C_DK
C_CHUNK = 64
D_GROUPS = ((128, 1), (512, 4), (2048, 16))
D_HEAD_DIM = 128
D_HEADS = (D_MODEL // 4) // D_HEAD_DIM
D_KV_WIDTH = D_HEADS * D_HEAD_DIM
D_Q_WIDTH = len(D_GROUPS) * D_KV_WIDTH
D_MAX_WINDOW = max(w for w, _ in D_GROUPS)
D_BLOCK = 128
FFN_DIM = 5632

EVEN_SIZES = (A_WIDTH, A_WIDTH, B_WIDTH, B_WIDTH, B_WIDTH, B_HEADS)
EVEN_IN = sum(EVEN_SIZES)
EVEN_OUT = A_WIDTH + B_WIDTH
ODD_SIZES = (C_WIDTH, C_WIDTH, C_WIDTH, C_WIDTH, D_Q_WIDTH, D_KV_WIDTH, D_KV_WIDTH)
ODD_IN = sum(ODD_SIZES)
ODD_OUT = C_WIDTH + D_KV_WIDTH

kernel_name = "hybrid_rglru_fox_hgrn2_dilated_decode_step"


def _split(z, sizes):
    return jnp.split(z, np.cumsum(sizes)[:-1].tolist(), axis=-1)


def rmsnorm(x, g):
    xf = x.astype(F32)
    y = xf * lax.rsqrt(jnp.mean(xf * xf, axis=-1, keepdims=True) + EPS)
    return (y * g.astype(F32)).astype(x.dtype)


def swiglu(x, wg, wu, wd):
    return (jax.nn.silu(x @ wg) * (x @ wu)) @ wd


def causal_conv(x, buf, w, b):
    t = x.shape[1]
    xp = jnp.concatenate([buf.astype(x.dtype), x], axis=1)
    y = b + xp[:, 0:t] * w[0]
    for j in range(1, CONV_W):
        y = y + xp[:, j:j + t] * w[j]
    return y, xp[:, -(CONV_W - 1):]


def _linear_combine(left, right):
    a_l, b_l = left
    a_r, b_r = right
    return a_l * a_r, a_r * b_l + b_r


def rg_lru(x, h0, starts_sequence, w_r, b_r, w_i, b_i, lam):
    n, t, _ = x.shape
    xf = x.astype(F32)
    xb = xf.reshape(n, t, A_HEADS, A_BLOCK)
    gate_r = jax.nn.sigmoid(jnp.einsum('nthi,hij->nthj', xb, w_r.astype(F32)).reshape(n, t, A_WIDTH) + b_r.astype(F32))
    gate_i = jax.nn.sigmoid(jnp.einsum('nthi,hij->nthj', xb, w_i.astype(F32)).reshape(n, t, A_WIDTH) + b_i.astype(F32))
    log_a = -LRU_C * gate_r * jax.nn.softplus(-lam.astype(F32))
    a = jnp.exp(log_a)
    mult = jnp.sqrt(-jnp.expm1(2.0 * log_a))
    if starts_sequence:
        mult = mult.at[:, 0].set(1.0)
    b = mult * gate_i * xf
    b = b.at[:, 0].add(a[:, 0] * h0.astype(F32))
    _, h = lax.associative_scan(_linear_combine, (a, b), axis=1)
    return h, h[:, -1]


def fox_prompt(q, k, v, logf):
    n, s, h, hd = q.shape
    nb = s // Q_BLOCK
    scale = hd ** -0.5
    cum = jnp.cumsum(logf, axis=1).transpose(0, 2, 1)
    qb = q.reshape(n, nb, Q_BLOCK, h, hd).transpose(1, 0, 2, 3, 4)
    cb = cum.reshape(n, h, nb, Q_BLOCK).transpose(2, 0, 1, 3)
    kpos = jnp.arange(s)

    def block(args):
        qi, ci, bi = args
        sc = jnp.einsum('nqhd,nkhd->nhqk', qi, k, preferred_element_type=F32) * scale
        sc = sc + ci[..., None] - cum[:, :, None, :]
        qpos = bi * Q_BLOCK + jnp.arange(Q_BLOCK)
        sc = jnp.where(kpos[None, :] <= qpos[:, None], sc, -jnp.inf)
        p = jax.nn.softmax(sc, axis=-1)
        return jnp.einsum('nhqk,nkhd->nqhd', p.astype(v.dtype), v)

    o = lax.map(block, (qb, cb, jnp.arange(nb)))
    return o.transpose(1, 0, 2, 3, 4).reshape(n, s, h, hd)


def fox_sample(q, k, v, logf, k_past, v_past, logf_past):
    n, t, h, hd = q.shape
    p_len = k_past.shape[1]
    scale = hd ** -0.5
    cum_past = jnp.cumsum(logf_past.astype(F32), axis=1).transpose(0, 2, 1)
    cum_new = cum_past[:, :, -1:] + jnp.cumsum(logf, axis=1).transpose(0, 2, 1)
    s_past = jnp.einsum('nthd,nphd->nhtp', q, k_past.astype(q.dtype), preferred_element_type=F32) * scale
    s_past = s_past + cum_new[..., None] - cum_past[:, :, None, :]
    s_new = jnp.einsum('nthd,nshd->nhts', q, k, preferred_element_type=F32) * scale
    s_new = s_new + cum_new[..., None] - cum_new[:, :, None, :]
    s_new = jnp.where(jnp.tril(jnp.ones((t, t), bool)), s_new, -jnp.inf)
    p = jax.nn.softmax(jnp.concatenate([s_past, s_new], axis=-1), axis=-1)
    o = jnp.einsum('nhtp,nphd->nthd', p[..., :p_len].astype(v.dtype), v_past.astype(v.dtype))
    return o + jnp.einsum('nhts,nshd->nthd', p[..., p_len:].astype(v.dtype), v)


def hgrn2_recurrence(q, k, v, logf, s0):
    n, t, h, _ = q.shape
    c = math.gcd(t, C_CHUNK)
    nc = t // c

    def to_chunks(z):
        return z.reshape(n, nc, c, h, z.shape[-1]).transpose(1, 0, 3, 2, 4)

    causal = jnp.tril(jnp.ones((c, c), bool))

    def step(s, inp):
        qc, kc, vc, gc = inp
        bcum = jnp.cumsum(gc, axis=2)
        o_inter = jnp.einsum('nhtk,nhkv->nhtv', qc * jnp.exp(bcum), s)
        diff = bcum[:, :, :, None, :] - bcum[:, :, None, :, :]
        decay = jnp.exp(jnp.where(causal[:, :, None], diff, -jnp.inf))
        att = jnp.einsum('nhtk,nhjk,nhtjk->nhtj', qc, kc, decay)
        o = o_inter + jnp.einsum('nhtj,nhjv->nhtv', att, vc)
        last = bcum[:, :, -1:, :]
        k_dec = kc * jnp.exp(last - bcum)
        s = jnp.exp(last[:, :, 0])[..., None] * s + jnp.einsum('nhjk,nhjv->nhkv', k_dec, vc)
        return s, o

    s_fin, o = lax.scan(step, s0, (to_chunks(q), to_chunks(k), to_chunks(v), to_chunks(logf)))
    o = o.transpose(1, 0, 3, 2, 4).reshape(n, t, h, -1)
    return o, s_fin


def dilated_group_prompt(q, k, v, band, dil):
    n, s, h, hd = q.shape
    scale = hd ** -0.5
    ln = s // dil
    nb = -(-ln // D_BLOCK)
    lp = nb * D_BLOCK

    def to_res(z):
        z = z.reshape(n, ln, dil, h, hd).transpose(0, 2, 1, 3, 4)
        return jnp.pad(z, ((0, 0), (0, 0), (0, lp - ln), (0, 0), (0, 0)))

    def to_band(z):
        zr = to_res(z)
        zp = jnp.pad(zr, ((0, 0), (0, 0), (D_BLOCK, 0), (0, 0), (0, 0)))
        prev = zp[:, :, :lp].reshape(n, dil, nb, D_BLOCK, h, hd)
        cur = zr.reshape(n, dil, nb, D_BLOCK, h, hd)
        return jnp.concatenate([prev, cur], axis=3)

    qb = to_res(q).reshape(n, dil, nb, D_BLOCK, h, hd)
    kb = to_band(k)
    vb = to_band(v)
    qi = np.arange(D_BLOCK)[:, None]
    kj = np.arange(2 * D_BLOCK)[None, :]
    dist = qi + D_BLOCK - kj
    start = np.arange(nb)[:, None, None] * D_BLOCK - D_BLOCK + kj
    mask = (dist >= 0) & (dist <= band) & (start >= 0)
    sc = jnp.einsum('brnqhd,brnkhd->brnhqk', qb, kb, preferred_element_type=F32) * scale
    sc = jnp.where(mask[None, None, :, None], sc, -jnp.inf)
    lse = jax.nn.logsumexp(sc, axis=-1)
    p = jnp.exp(sc - lse[..., None])
    o = jnp.einsum('brnhqk,brnkhd->brnqhd', p, vb.astype(F32))
    o = o.reshape(n, dil, lp, h, hd)[:, :, :ln].transpose(0, 2, 1, 3, 4).reshape(n, s, h, hd)
    lse = lse.transpose(0, 1, 2, 4, 3).reshape(n, dil, lp, h)[:, :, :ln].transpose(0, 2, 1, 3).reshape(n, s, h)
    return o, lse


def dilated_group_sample(q, k_all, v_all, window, dil, wbuf):
    t = q.shape[1]
    scale = q.shape[-1] ** -0.5
    m = np.arange(window // dil + 1)
    idx = wbuf + np.arange(t)[:, None] - m[None, :] * dil
    valid = idx >= 0
    idx = np.maximum(idx, 0)
    kg = k_all[:, idx]
    vg = v_all[:, idx]
    sc = jnp.einsum('nthd,ntmhd->nhtm', q, kg, preferred_element_type=F32) * scale
    sc = jnp.where(valid[None, None], sc, -jnp.inf)
    lse = jax.nn.logsumexp(sc, axis=-1)
    p = jnp.exp(sc - lse[..., None])
    o = jnp.einsum('nhtm,ntmhd->nthd', p, vg.astype(F32))
    return o, lse.transpose(0, 2, 1)


def even_mixer(z, conv_buf, h0, starts_sequence, fox_past, conv_w, conv_b, w_r, b_r, w_i, b_i, lam, b_f):
    n, t, _ = z.shape
    ax, ag, bq, bk, bv, bf = _split(z, EVEN_SIZES)
    xc, conv_state = causal_conv(ax, conv_buf, conv_w, conv_b)
    h, h_last = rg_lru(xc, h0, starts_sequence, w_r, b_r, w_i, b_i, lam)
    ya = h.astype(z.dtype) * jax.nn.gelu(ag)
    q = bq.reshape(n, t, B_HEADS, B_HEAD_DIM)
    k = bk.reshape(n, t, B_HEADS, B_HEAD_DIM)
    v = bv.reshape(n, t, B_HEADS, B_HEAD_DIM)
    logf = jax.nn.log_sigmoid((bf + b_f).astype(F32))
    if fox_past is None:
        ob = fox_prompt(q, k, v, logf)
    else:
        ob = fox_sample(q, k, v, logf, fox_past[0], fox_past[1], fox_past[2])
    y = jnp.concatenate([ya, ob.reshape(n, t, B_WIDTH).astype(z.dtype)], axis=-1)
    return y, (k, v, logf, conv_state, h_last)


def odd_mixer(z, s0, lb, dsw_past, g_norm):
    n, t, _ = z.shape
    cq, cf, ci, cg, dq, dk, dv = _split(z, ODD_SIZES)
    cf32 = cf.astype(F32)
    logf = jnp.logaddexp(jnp.log(lb), jnp.log1p(-lb) + jax.nn.log_sigmoid(cf32))
    kk = (1.0 - lb) * jax.nn.sigmoid(-cf32)

    def heads(a):
        return a.astype(F32).reshape(n, t, C_HEADS, -1)

    oc, s_new = hgrn2_recurrence(heads(cq), heads(kk), heads(ci), heads(logf), s0.astype(F32))
    oc = rmsnorm(oc, g_norm).reshape(n, t, C_WIDTH) * jax.nn.silu(cg.astype(F32))
    qd = dq.reshape(n, t, len(D_GROUPS), D_HEADS, D_HEAD_DIM)
    kd = dk.reshape(n, t, D_HEADS, D_HEAD_DIM)
    vd = dv.reshape(n, t, D_HEADS, D_HEAD_DIM)
    outs = []
    if dsw_past is None:
        for g, (w, d) in enumerate(D_GROUPS):
            outs.append(dilated_group_prompt(qd[:, :, g], kd, vd, w // d, d))
        keep = min(D_MAX_WINDOW, t)
        kv_state = (kd[:, -keep:], vd[:, -keep:])
    else:
        wbuf = dsw_past[0].shape[1]
        k_all = jnp.concatenate([dsw_past[0].astype(kd.dtype), kd], axis=1)
        v_all = jnp.concatenate([dsw_past[1].astype(vd.dtype), vd], axis=1)
        for g, (w, d) in enumerate(D_GROUPS):
            outs.append(dilated_group_sample(qd[:, :, g], k_all, v_all, w, d, wbuf))
        kv_state = (kd, vd)
    o_g = jnp.stack([o for o, _ in outs])
    wgt = jax.nn.softmax(jnp.stack([l for _, l in outs]), axis=0)
    od = jnp.sum(wgt[..., None] * o_g, axis=0).reshape(n, t, D_KV_WIDTH)
    y = jnp.concatenate([oc.astype(z.dtype), od.astype(z.dtype)], axis=-1)
    return y, (s_new, kv_state[0], kv_state[1])


def setup_inputs(seed: int = 0) -> dict:
    key = jax.random.key(seed)
    ks = iter(jax.random.split(key, 48))

    def nrm(shape, scale=1.0):
        return scale * jax.random.normal(next(ks), shape, F32)

    n_pages = PAST_LEN // PAGE_SIZE
    n_used = DEC_BATCH * n_pages
    n_pool = n_used + max(1, n_used // 4)
    wbuf = min(D_MAX_WINDOW, PAST_LEN)
    page_table = jax.random.permutation(next(ks), n_pool)[:n_used].reshape(DEC_BATCH, n_pages).astype(jnp.int32)
    u = jax.random.uniform(next(ks), (N_EVEN, A_WIDTH), F32, 0.9, 0.999)
    a0 = u ** (1.0 / LRU_C)
    lru_lambda = jnp.log(a0) - jnp.log1p(-a0)
    return {
        'x_prompt': nrm((BATCH, SEQ, D_MODEL)),
        'x_sample': nrm((DEC_BATCH, DEC_SEQ, D_MODEL)),
        'cache_fox_k': nrm((N_EVEN, n_pool, PAGE_SIZE, B_HEADS, B_HEAD_DIM)),
        'cache_fox_v': nrm((N_EVEN, n_pool, PAGE_SIZE, B_HEADS, B_HEAD_DIM)),
        'cache_fox_logf': jax.nn.log_sigmoid(3.0 + nrm((N_EVEN, n_pool, PAGE_SIZE, B_HEADS))),
        'page_table': page_table,
        'state_lru_conv': nrm((N_EVEN, DEC_BATCH, CONV_W - 1, A_WIDTH)),
        'state_lru_h': nrm((N_EVEN, DEC_BATCH, A_WIDTH), 0.5),
        'state_hgrn': nrm((N_ODD, DEC_BATCH, C_HEADS, C_DK, C_DV), 0.5),
        'cache_dsw_k': nrm((N_ODD, DEC_BATCH, wbuf, D_HEADS, D_HEAD_DIM)),
        'cache_dsw_v': nrm((N_ODD, DEC_BATCH, wbuf, D_HEADS, D_HEAD_DIM)),
        'norm_gains': 1.0 + nrm((DEPTH, 3, D_MODEL), 0.02),
        'ffn_w_gate': nrm((DEPTH, 2, D_MODEL, FFN_DIM), D_MODEL ** -0.5),
        'ffn_w_up': nrm((DEPTH, 2, D_MODEL, FFN_DIM), D_MODEL ** -0.5),
        'ffn_w_down': nrm((DEPTH, 2, FFN_DIM, D_MODEL), FFN_DIM ** -0.5),
        'even_w_in': nrm((N_EVEN, D_MODEL, EVEN_IN), D_MODEL ** -0.5),
        'even_w_out': nrm((N_EVEN, EVEN_OUT, D_MODEL), EVEN_OUT ** -0.5),
        'lru_conv_w': nrm((N_EVEN, CONV_W, A_WIDTH), CONV_W ** -0.5),
        'lru_conv_b': nrm((N_EVEN, A_WIDTH), 0.02),
        'lru_w_r': nrm((N_EVEN, A_HEADS, A_BLOCK, A_BLOCK), A_BLOCK ** -0.5),
        'lru_b_r': nrm((N_EVEN, A_WIDTH), 0.02),
        'lru_w_i': nrm((N_EVEN, A_HEADS, A_BLOCK, A_BLOCK), A_BLOCK ** -0.5),
        'lru_b_i': nrm((N_EVEN, A_WIDTH), 0.02),
        'lru_lambda': lru_lambda,
        'fox_b_f': jax.random.uniform(next(ks), (N_EVEN, B_HEADS), F32, 1.0, 4.0),
        'odd_w_in': nrm((N_ODD, D_MODEL, ODD_IN), D_MODEL ** -0.5),
        'odd_w_out': nrm((N_ODD, ODD_OUT, D_MODEL), ODD_OUT ** -0.5),
        'hgrn_lb_logits': nrm((DEPTH, C_WIDTH), 0.5),
        'hgrn_norm': 1.0 + nrm((N_ODD, C_DV), 0.02),
        'final_norm': 1.0 + nrm((D_MODEL,), 0.02),
    }


def reference(x_prompt, x_sample, cache_fox_k, cache_fox_v, cache_fox_logf, page_table,
              state_lru_conv, state_lru_h, state_hgrn, cache_dsw_k, cache_dsw_v,
              norm_gains, ffn_w_gate, ffn_w_up, ffn_w_down,
              even_w_in, even_w_out, lru_conv_w, lru_conv_b, lru_w_r, lru_b_r, lru_w_i, lru_b_i,
              lru_lambda, fox_b_f, odd_w_in, odd_w_out, hgrn_lb_logits, hgrn_norm, final_norm):
    n_p = x_prompt.shape[0]
    n_s = x_sample.shape[0]
    lb_all = jnp.cumsum(jax.nn.softmax(hgrn_lb_logits.astype(F32), axis=0), axis=0)
    lb_all = lb_all - lb_all[:1]
    names = ('fox_k', 'fox_v', 'fox_logf', 'lru_conv', 'lru_h', 'hgrn', 'dsw_k', 'dsw_v')
    st_p = {nm: [] for nm in names}
    st_s = {nm: [] for nm in names}
    xp, xs = x_prompt, x_sample
    for l in range(DEPTH):
        j = l // 2
        xp = xp + 0.5 * swiglu(rmsnorm(xp, norm_gains[l, 0]), ffn_w_gate[l, 0], ffn_w_up[l, 0], ffn_w_down[l, 0])
        xs = xs + 0.5 * swiglu(rmsnorm(xs, norm_gains[l, 0]), ffn_w_gate[l, 0], ffn_w_up[l, 0], ffn_w_down[l, 0])
        if l % 2 == 0:
            zp = rmsnorm(xp, norm_gains[l, 1]) @ even_w_in[j]
            zs = rmsnorm(xs, norm_gains[l, 1]) @ even_w_in[j]
            yp, (k_p, v_p, lf_p, cv_p, h_p) = even_mixer(
                zp, jnp.zeros((n_p, CONV_W - 1, A_WIDTH), zp.dtype), jnp.zeros((n_p, A_WIDTH), F32), True, None,
                lru_conv_w[j], lru_conv_b[j], lru_w_r[j], lru_b_r[j], lru_w_i[j], lru_b_i[j], lru_lambda[j], fox_b_f[j])
            past = (cache_fox_k[j][page_table].reshape(n_s, -1, B_HEADS, B_HEAD_DIM),
                    cache_fox_v[j][page_table].reshape(n_s, -1, B_HEADS, B_HEAD_DIM),
                    cache_fox_logf[j][page_table].reshape(n_s, -1, B_HEADS))
            ys, (k_s, v_s, lf_s, cv_s, h_s) = even_mixer(
                zs, state_lru_conv[j], state_lru_h[j], False, past,
                lru_conv_w[j], lru_conv_b[j], lru_w_r[j], lru_b_r[j], lru_w_i[j], lru_b_i[j], lru_lambda[j], fox_b_f[j])
            xp = xp + yp @ even_w_out[j]
            xs = xs + ys @ even_w_out[j]
            for st, vals in ((st_p, (k_p, v_p, lf_p, cv_p, h_p)), (st_s, (k_s, v_s, lf_s, cv_s, h_s))):
                for nm, val in zip(('fox_k', 'fox_v', 'fox_logf', 'lru_conv', 'lru_h'), vals):
                    st[nm].append(val)
        else:
            zp = rmsnorm(xp, norm_gains[l, 1]) @ odd_w_in[j]
            zs = rmsnorm(xs, norm_gains[l, 1]) @ odd_w_in[j]
            yp, (s_p, dk_p, dv_p) = odd_mixer(zp, jnp.zeros((n_p, C_HEADS, C_DK, C_DV), F32), lb_all[l], None, hgrn_norm[j])
            ys, (s_s, dk_s, dv_s) = odd_mixer(zs, state_hgrn[j], lb_all[l], (cache_dsw_k[j], cache_dsw_v[j]), hgrn_norm[j])
            xp = xp + yp @ odd_w_out[j]
            xs = xs + ys @ odd_w_out[j]
            for st, vals in ((st_p, (s_p, dk_p, dv_p)), (st_s, (s_s, dk_s, dv_s))):
                for nm, val in zip(('hgrn', 'dsw_k', 'dsw_v'), vals):
                    st[nm].append(val)
        xp = xp + 0.5 * swiglu(rmsnorm(xp, norm_gains[l, 2]), ffn_w_gate[l, 1], ffn_w_up[l, 1], ffn_w_down[l, 1])
        xs = xs + 0.5 * swiglu(rmsnorm(xs, norm_gains[l, 2]), ffn_w_gate[l, 1], ffn_w_up[l, 1], ffn_w_down[l, 1])
    y_prompt = rmsnorm(xp, final_norm)
    y_sample = rmsnorm(xs, final_norm)
    fox_k_p = jnp.stack(st_p['fox_k'])
    fox_v_p = jnp.stack(st_p['fox_v'])
    fox_logf_p = jnp.stack(st_p['fox_logf'])
    lru_conv_p = jnp.stack(st_p['lru_conv'])
    lru_h_p = jnp.stack(st_p['lru_h'])
    hgrn_p = jnp.stack(st_p['hgrn'])
    dsw_k_p = jnp.stack(st_p['dsw_k'])
    dsw_v_p = jnp.stack(st_p['dsw_v'])
    fox_k_s = jnp.stack(st_s['fox_k'])
    fox_v_s = jnp.stack(st_s['fox_v'])
    fox_logf_s = jnp.stack(st_s['fox_logf'])
    lru_conv_s = jnp.stack(st_s['lru_conv'])
    lru_h_s = jnp.stack(st_s['lru_h'])
    hgrn_s = jnp.stack(st_s['hgrn'])
    dsw_k_s = jnp.stack(st_s['dsw_k'])
    dsw_v_s = jnp.stack(st_s['dsw_v'])
    return (y_prompt, y_sample,
            fox_k_p, fox_v_p, fox_logf_p, lru_conv_p, lru_h_p, hgrn_p, dsw_k_p, dsw_v_p,
            fox_k_s, fox_v_s, fox_logf_s, lru_conv_s, lru_h_s, hgrn_s, dsw_k_s, dsw_v_s)
```

```python
import functools

import numpy as np
import jax
import jax.numpy as jnp
from jax import lax
from jax.experimental import pallas as pl
from jax.experimental.pallas import tpu as pltpu

F32 = jnp.float32
BF16 = jnp.bfloat16
EPS = 1e-6
HD = 128
LRU_C = 8.0
CONV_W = 4
D_GROUPS = ((128, 1), (512, 4), (2048, 16))
VMEM_LIMIT_BYTES = 56 * 1024 * 1024
HIGHEST = lax.Precision.HIGHEST
NT_DIMS = (((1,), (1,)), ((), ()))


def _params(*sem):
    return pltpu.CompilerParams(dimension_semantics=sem, vmem_limit_bytes=VMEM_LIMIT_BYTES)


def _rms(x, g):
    ms = jnp.mean(x * x, axis=-1, keepdims=True)
    return x * lax.rsqrt(ms + EPS) * g


def _softplus(z):
    return jnp.maximum(z, 0.0) + jnp.log1p(jnp.exp(-jnp.abs(z)))


def _log_sigmoid(z):
    return -_softplus(-z)


def _dot_nt(a, b):
    return lax.dot_general(a, b, NT_DIMS, preferred_element_type=F32)


def _col_from_row(row, eye):
    return jnp.sum(eye * row, axis=1, keepdims=True)


def _ffn_kernel(x_ref, g_ref, wg_ref, wu_ref, wd_ref, gf_ref, o_ref, xn_ref, *, nf, final):
    f = pl.program_id(1)

    @pl.when(f == 0)
    def _():
        xn_ref[...] = _rms(x_ref[...], g_ref[...]).astype(BF16)
        o_ref[...] = jnp.zeros_like(o_ref)

    xn = xn_ref[...]
    a = jnp.dot(xn, wg_ref[...], preferred_element_type=F32)
    b = jnp.dot(xn, wu_ref[...], preferred_element_type=F32)
    h = (a * jax.nn.sigmoid(a) * b).astype(BF16)
    o_ref[...] += jnp.dot(h, wd_ref[...], preferred_element_type=F32)

    @pl.when(f == nf - 1)
    def _():
        y = x_ref[...] + 0.5 * o_ref[...]
        if final:
            y = _rms(y, gf_ref[...])
        o_ref[...] = y


def ffn(x, g, wg, wu, wd, gf, *, final, tm, tf):
    m, d = x.shape
    nf = wg.shape[1] // tf
    return pl.pallas_call(
        functools.partial(_ffn_kernel, nf=nf, final=final),
        grid=(m // tm, nf),
        in_specs=[
            pl.BlockSpec((tm, d), lambda i, f: (i, 0)),
            pl.BlockSpec((1, d), lambda i, f: (0, 0)),
            pl.BlockSpec((d, tf), lambda i, f: (0, f)),
            pl.BlockSpec((d, tf), lambda i, f: (0, f)),
            pl.BlockSpec((tf, d), lambda i, f: (f, 0)),
            pl.BlockSpec((1, d), lambda i, f: (0, 0)),
        ],
        out_specs=pl.BlockSpec((tm, d), lambda i, f: (i, 0)),
        out_shape=jax.ShapeDtypeStruct((m, d), F32),
        scratch_shapes=[pltpu.VMEM((tm, d), BF16)],
        compiler_params=_params("parallel", "arbitrary"),
        name="ffn",
    )(x, g.reshape(1, d), wg, wu, wd, gf.reshape(1, d))


def _norm_matmul_kernel(x_ref, g_ref, w_ref, o_ref, xn_ref):
    @pl.when(pl.program_id(1) == 0)
    def _():
        xn_ref[...] = _rms(x_ref[...], g_ref[...]).astype(BF16)

    o_ref[...] = jnp.dot(xn_ref[...], w_ref[...], preferred_element_type=F32)


def norm_matmul(x, g, w, *, tm, tn):
    m, d = x.shape
    n = w.shape[1]
    return pl.pallas_call(
        _norm_matmul_kernel,
        grid=(m // tm, n // tn),
        in_specs=[
            pl.BlockSpec((tm, d), lambda i, j: (i, 0)),
            pl.BlockSpec((1, d), lambda i, j: (0, 0)),
            pl.BlockSpec((d, tn), lambda i, j: (0, j)),
        ],
        out_specs=pl.BlockSpec((tm, tn), lambda i, j: (i, j)),
        out_shape=jax.ShapeDtypeStruct((m, n), F32),
        scratch_shapes=[pltpu.VMEM((tm, d), BF16)],
        compiler_params=_params("parallel", "arbitrary"),
        name="norm_matmul",
    )(x, g.reshape(1, d), w)


def _out_proj_kernel(r_ref, ya_ref, yb_ref, wa_ref, wb_ref, o_ref):
    acc = jnp.dot(ya_ref[...].astype(BF16), wa_ref[...], preferred_element_type=F32)
    acc = acc + jnp.dot(yb_ref[...].astype(BF16), wb_ref[...], preferred_element_type=F32)
    o_ref[...] = r_ref[...] + acc


def out_proj(res, ya, yb, wa, wb, *, tm, tn):
    m, d = res.shape
    ka, kb = ya.shape[1], yb.shape[1]
    return pl.pallas_call(
        _out_proj_kernel,
        grid=(m // tm, d // tn),
        in_specs=[
            pl.BlockSpec((tm, tn), lambda i, j: (i, j)),
            pl.BlockSpec((tm, ka), lambda i, j: (i, 0)),
            pl.BlockSpec((tm, kb), lambda i, j: (i, 0)),
            pl.BlockSpec((ka, tn), lambda i, j: (0, j)),
            pl.BlockSpec((kb, tn), lambda i, j: (0, j)),
        ],
        out_specs=pl.BlockSpec((tm, tn), lambda i, j: (i, j)),
        out_shape=jax.ShapeDtypeStruct((m, d), F32),
        compiler_params=_params("parallel", "arbitrary"),
        name="out_proj",
    )(res, ya, yb, wa, wb)


def _lru_kernel(ax_ref, ag_ref, cb0_ref, h0_ref, cw_ref, cb_ref, wri_ref, br_ref, bi_ref, lam_ref,
                ya_ref, cst_ref, hl_ref, xbuf, hcar, a_s, b_s, *, tt, nh, starts):
    tb = pl.program_id(1)

    @pl.when(tb == 0)
    def _():
        xbuf[5:8, :] = cb0_ref[...]
        hcar[...] = h0_ref[...]

    x = ax_ref[...]
    xbuf[8:8 + tt, :] = x
    cw = cw_ref[...]
    y = cb_ref[...] + xbuf[5:5 + tt, :] * cw[0:1, :]
    y = y + xbuf[6:6 + tt, :] * cw[1:2, :]
    y = y + xbuf[7:7 + tt, :] * cw[2:3, :]
    y = y + x * cw[3:4, :]
    tail = xbuf[5 + tt:8 + tt, :]
    xbuf[5:8, :] = tail
    cst_ref[...] = tail

    sp = _softplus(-lam_ref[...])
    for h in range(nh):
        sl = slice(h * HD, (h + 1) * HD)
        yh = y[:, sl]
        gts = jnp.dot(yh.astype(BF16), wri_ref[h], preferred_element_type=F32)
        gr = jax.nn.sigmoid(gts[:, :HD] + br_ref[:, sl])
        gi = jax.nn.sigmoid(gts[:, HD:] + bi_ref[:, sl])
        log_a = -LRU_C * gr * sp[:, sl]
        a = jnp.exp(log_a)
        mult = jnp.sqrt(-jnp.tanh(log_a) * (a * a + 1.0))
        if starts:
            row = lax.broadcasted_iota(jnp.int32, mult.shape, 0)
            mult = jnp.where((row == 0) & (tb == 0), 1.0, mult)
        a_s[:, sl] = a
        b_s[:, sl] = mult * gi * yh

    def group(gidx, hprev):
        r = pl.multiple_of(gidx * 8, 8)
        a8 = a_s[pl.ds(r, 8), :]
        b8 = b_s[pl.ds(r, 8), :]
        row = lax.broadcasted_iota(jnp.int32, a8.shape, 0)
        for s in (1, 2, 4):
            ok = row >= s
            b8 = jnp.where(ok, a8 * pltpu.roll(b8, s, 0) + b8, b8)
            a8 = jnp.where(ok, a8 * pltpu.roll(a8, s, 0), a8)
        h8 = a8 * hprev + b8
        b_s[pl.ds(r, 8), :] = h8
        return h8[7:8, :]

    hlast = lax.fori_loop(0, tt // 8, group, hcar[...])
    hcar[...] = hlast
    hl_ref[...] = hlast
    ya_ref[...] = b_s[...] * jax.nn.gelu(ag_ref[...])


def lru_mixer(z, conv0, h0, cw, cb, wri, br, bi, lam, *, n, t, tt, starts):
    w = lam.shape[-1]
    nh = w // HD
    nb = t // tt
    row = lambda i, b: (i * nb + b, 0)
    vec = pl.BlockSpec((1, w), lambda i, b: (0, 0))
    return pl.pallas_call(
        functools.partial(_lru_kernel, tt=tt, nh=nh, starts=starts),
        grid=(n, nb),
        in_specs=[
            pl.BlockSpec((tt, w), row),
            pl.BlockSpec((tt, w), lambda i, b: (i * nb + b, 1)),
            pl.BlockSpec((None, CONV_W - 1, w), lambda i, b: (i, 0, 0)),
            pl.BlockSpec((None, 1, w), lambda i, b: (i, 0, 0)),
            pl.BlockSpec((CONV_W, w), lambda i, b: (0, 0)),
            vec,
            pl.BlockSpec((nh, HD, 2 * HD), lambda i, b: (0, 0, 0)),
            vec, vec, vec,
        ],
        out_specs=[
            pl.BlockSpec((tt, w), row),
            pl.BlockSpec((None, CONV_W - 1, w), lambda i, b: (i, 0, 0)),
            pl.BlockSpec((None, 1, w), lambda i, b: (i, 0, 0)),
        ],
        out_shape=[
            jax.ShapeDtypeStruct((n * t, w), F32),
            jax.ShapeDtypeStruct((n, CONV_W - 1, w), F32),
            jax.ShapeDtypeStruct((n, 1, w), F32),
        ],
        scratch_shapes=[
            pltpu.VMEM((tt + 8, w), F32),
            pltpu.VMEM((1, w), F32),
            pltpu.VMEM((tt, w), F32),
            pltpu.VMEM((tt, w), F32),
        ],
        compiler_params=_params("parallel", "arbitrary"),
        name="lru_mixer",
    )(z, z, conv0, h0.reshape(n, 1, w), cw, cb.reshape(1, w), wri,
      br.reshape(1, w), bi.reshape(1, w), lam.reshape(1, w))


def _fox_prep_kernel(zf_ref, bf_ref, lf_ref, cum_ref, cumt_ref, cum_s, cumt_s, *, s_len, nh):
    lf = _log_sigmoid(zf_ref[...] + bf_ref[...])
    lf_ref[...] = lf[:, :nh]
    cum_s[...] = lf
    r = lax.broadcasted_iota(jnp.int32, (HD, HD), 0)
    c = lax.broadcasted_iota(jnp.int32, (HD, HD), 1)
    tri = (r >= c).astype(F32)

    def chunk(i, carry):
        st = pl.multiple_of(i * HD, HD)
        cs = jnp.dot(tri, cum_s[pl.ds(st, HD), :], preferred_element_type=F32, precision=HIGHEST) + carry
        cum_s[pl.ds(st, HD), :] = cs
        cumt_s[:, pl.ds(st, HD)] = cs.T
        return cs[HD - 1:HD, :]

    lax.fori_loop(0, s_len // HD, chunk, jnp.zeros((1, HD), F32))
    cum_ref[...] = cum_s[:, :nh]
    cumt_ref[...] = cumt_s[:nh, :]


def fox_prep(zf, bfp, *, n, s_len, nh):
    return pl.pallas_call(
        functools.partial(_fox_prep_kernel, s_len=s_len, nh=nh),
        grid=(n,),
        in_specs=[
            pl.BlockSpec((s_len, HD), lambda i: (i, 0)),
            pl.BlockSpec((1, HD), lambda i: (0, 0)),
        ],
        out_specs=[
            pl.BlockSpec((s_len, nh), lambda i: (i, 0)),
            pl.BlockSpec((s_len, nh), lambda i: (i, 0)),
            pl.BlockSpec((None, nh, s_len), lambda i: (i, 0, 0)),
        ],
        out_shape=[
            jax.ShapeDtypeStruct((n * s_len, nh), F32),
            jax.ShapeDtypeStruct((n * s_len, nh), F32),
            jax.ShapeDtypeStruct((n, nh, s_len), F32),
        ],
        scratch_shapes=[pltpu.VMEM((s_len, HD), F32), pltpu.VMEM((HD, s_len), F32)],
        compiler_params=_params("parallel"),
        name="fox_prep",
    )(zf, bfp)


def _fox_prompt_kernel(q_ref, k_ref, v_ref, cq_ref, ck_ref, o_ref, m_s, l_s, acc_s, *, tq, nh, scale):
    qi = pl.program_id(1)
    kj = pl.program_id(2)

    @pl.when(kj == 0)
    def _():
        m_s[...] = jnp.full_like(m_s, -jnp.inf)
        l_s[...] = jnp.zeros_like(l_s)
        acc_s[...] = jnp.zeros_like(acc_s)

    @pl.when(kj <= qi)
    def _():
        qpos = qi * tq + lax.broadcasted_iota(jnp.int32, (tq, tq), 0)
        kpos = kj * tq + lax.broadcasted_iota(jnp.int32, (tq, tq), 1)
        causal = kpos <= qpos
        for h in range(nh):
            sl = slice(h * HD, (h + 1) * HD)
            s = _dot_nt(q_ref[:, sl].astype(BF16), k_ref[:, sl].astype(BF16)) * scale
            s = s + cq_ref[:, h:h + 1] - ck_ref[h:h + 1, :]
            s = jnp.where(causal, s, -jnp.inf)
            m_prev = m_s[h]
            m_new = jnp.maximum(m_prev, jnp.max(s, axis=-1, keepdims=True))
            alpha = jnp.exp(m_prev - m_new)
            p = jnp.exp(s - m_new)
            l_s[h] = alpha * l_s[h] + jnp.sum(p, axis=-1, keepdims=True)
            acc_s[:, sl] = alpha * acc_s[:, sl] + jnp.dot(
                p.astype(BF16), v_ref[:, sl].astype(BF16), preferred_element_type=F32)
            m_s[h] = m_new

    @pl.when(kj == qi)
    def _():
        for h in range(nh):
            sl = slice(h * HD, (h + 1) * HD)
            o_ref[:, sl] = acc_s[:, sl] / l_s[h]


def fox_prompt(z, cum, cumt, *, n, s_len, tq, qcol):
    nh = cum.shape[1]
    w = nh * HD
    nq = s_len // tq
    return pl.pallas_call(
        functools.partial(_fox_prompt_kernel, tq=tq, nh=nh, scale=HD ** -0.5),
        grid=(n, nq, nq),
        in_specs=[
            pl.BlockSpec((tq, w), lambda i, a, b: (i * nq + a, qcol)),
            pl.BlockSpec((tq, w), lambda i, a, b: (i * nq + jnp.minimum(a, b), qcol + 1)),
            pl.BlockSpec((tq, w), lambda i, a, b: (i * nq + jnp.minimum(a, b), qcol + 2)),
            pl.BlockSpec((tq, nh), lambda i, a, b: (i * nq + a, 0)),
            pl.BlockSpec((None, nh, tq), lambda i, a, b: (i, 0, jnp.minimum(a, b))),
        ],
        out_specs=pl.BlockSpec((tq, w), lambda i, a, b: (i * nq + a, 0)),
        out_shape=jax.ShapeDtypeStruct((n * s_len, w), F32),
        scratch_shapes=[
            pltpu.VMEM((nh, tq, 1), F32),
            pltpu.VMEM((nh, tq, 1), F32),
            pltpu.VMEM((tq, w), F32),
        ],
        compiler_params=_params("parallel", "parallel", "arbitrary"),
        name="fox_prompt",
    )(z, z, z, cum, cumt)


def _fox_sample_kernel(pt_ref, q_ref, kn_ref, vn_ref, lfn_ref, *refs, pp, t, nh, scale):
    kc, vc, lc = refs[0:pp], refs[pp:2 * pp], refs[2 * pp:3 * pp]
    o_ref = refs[3 * pp]
    qb_s, crow_s, carry_s, m_s, l_s, acc_s = refs[3 * pp + 1:]
    j = pl.program_id(1)
    nj = pl.num_programs(1)
    ncol = nh * t

    r128 = lax.broadcasted_iota(jnp.int32, (HD, HD), 0)
    c128 = lax.broadcasted_iota(jnp.int32, (HD, HD), 1)
    eye = (r128 == c128).astype(F32)
    tri_suffix = (c128 > r128).astype(F32)
    hrow = lax.broadcasted_iota(jnp.int32, (nh, HD), 0)
    hcol = lax.broadcasted_iota(jnp.int32, (nh, HD), 1)
    expand = ((hcol // t == hrow) & (hcol < ncol)).astype(F32)

    def page(k_pg, v_pg, lf_pg, valid):
        lf_exp = jnp.dot(lf_pg, expand, preferred_element_type=F32, precision=HIGHEST)
        later = jnp.dot(tri_suffix, lf_exp, preferred_element_type=F32, precision=HIGHEST) + carry_s[...]
        carry_s[...] = carry_s[...] + jnp.sum(lf_exp, axis=0, keepdims=True)
        s = _dot_nt(k_pg.astype(BF16), qb_s[...]) * scale + later + crow_s[...]
        if valid is not None:
            s = jnp.where(valid, s, -jnp.inf)
        m_prev = m_s[...]
        m_new = jnp.maximum(m_prev, jnp.max(s, axis=0, keepdims=True))
        alpha = jnp.exp(m_prev - m_new)
        p = jnp.exp(s - m_new)
        l_s[...] = alpha * l_s[...] + jnp.sum(p, axis=0, keepdims=True)
        m_s[...] = m_new
        acc_s[...] = _col_from_row(alpha, eye) * acc_s[...] + jnp.dot(
            p.T.astype(BF16), v_pg.astype(BF16), preferred_element_type=F32)

    @pl.when(j == 0)
    def _():
        q = q_ref[...]
        w = q.shape[1]
        qrep = jnp.concatenate([q] * nh + [jnp.zeros((HD - ncol, w), F32)], axis=0)
        rr = lax.broadcasted_iota(jnp.int32, (HD, w), 0)
        cc = lax.broadcasted_iota(jnp.int32, (HD, w), 1)
        qb_s[...] = jnp.where(rr // t == cc // HD, qrep, 0.0).astype(BF16)
        m_s[...] = jnp.full_like(m_s, -jnp.inf)
        l_s[...] = jnp.zeros_like(l_s)
        acc_s[...] = jnp.zeros_like(acc_s)
        carry_s[...] = jnp.zeros_like(carry_s)
        lfn = lfn_ref[...]
        lfn_exp = jnp.dot(lfn, expand, preferred_element_type=F32, precision=HIGHEST)
        after = r128 > (c128 % t)
        crow_s[...] = -jnp.sum(jnp.where(after, lfn_exp, 0.0), axis=0, keepdims=True)
        page(kn_ref[...], vn_ref[...], lfn, r128 <= (c128 % t))

    @pl.when(j > 0)
    def _():
        for i in range(pp):
            page(kc[i][...], vc[i][...], lc[i][...], None)

    @pl.when(j == nj - 1)
    def _():
        lcol = _col_from_row(l_s[...], eye)
        for h in range(nh):
            o_ref[:, h * HD:(h + 1) * HD] = (
                acc_s[h * t:(h + 1) * t, h * HD:(h + 1) * HD] / lcol[h * t:(h + 1) * t, :])


def fox_sample(page_table, zs, kn_pad, vn_pad, lfn_pad, cache_k, cache_v, cache_lf, *, n, t, nh, qcol, pp):
    w = nh * HD
    n_pages = page_table.shape[1]
    nj = 1 + n_pages // pp

    def pg(i):
        return lambda b, j, pt: (pt[b, n_pages - 1 - (jnp.maximum(j, 1) - 1) * pp - i], 0, 0)

    in_specs = [
        pl.BlockSpec((t, w), lambda b, j, pt: (b, qcol)),
        pl.BlockSpec((None, HD, w), lambda b, j, pt: (b, 0, 0)),
        pl.BlockSpec((None, HD, w), lambda b, j, pt: (b, 0, 0)),
        pl.BlockSpec((None, HD, nh), lambda b, j, pt: (b, 0, 0)),
    ]
    in_specs += [pl.BlockSpec((None, HD, w), pg(i)) for i in range(pp)]
    in_specs += [pl.BlockSpec((None, HD, w), pg(i)) for i in range(pp)]
    in_specs += [pl.BlockSpec((None, HD, nh), pg(i)) for i in range(pp)]
    return pl.pallas_call(
        functools.partial(_fox_sample_kernel, pp=pp, t=t, nh=nh, scale=HD ** -0.5),
        grid_spec=pltpu.PrefetchScalarGridSpec(
            num_scalar_prefetch=1,
            grid=(n, nj),
            in_specs=in_specs,
            out_specs=pl.BlockSpec((t, w), lambda b, j, pt: (b, 0)),
            scratch_shapes=[
                pltpu.VMEM((HD, w), BF16),
                pltpu.VMEM((1, HD), F32),
                pltpu.VMEM((1, HD), F32),
                pltpu.VMEM((1, HD), F32),
                pltpu.VMEM((1, HD), F32),
                pltpu.VMEM((HD, w), F32),
            ],
        ),
        out_shape=jax.ShapeDtypeStruct((n * t, w), F32),
        compiler_params=_params("parallel", "arbitrary"),
        name="fox_sample",
    )(page_table, zs, kn_pad, vn_pad, lfn_pad, *([cache_k] * pp), *([cache_v] * pp), *([cache_lf] * pp))


HG_CHUNK = 128
HG_LEVELS = 7


def _hgrn_sum_matrix():
    c = HG_CHUNK
    t = np.arange(c)[:, None]
    i = np.arange(c)[None, :]
    blocks = [i <= t, i > t]
    for lv in range(1, HG_LEVELS + 1):
        s = 1 << lv
        half = s // 2
        start = (t // s) * s
        upper = (t % s) >= half
        blocks.append(upper & (i >= start + half) & (i <= t))
        blocks.append((~upper) & (i > t) & (i <= start + half - 1))
    d = np.concatenate(blocks, axis=0).astype(np.float32)
    return np.concatenate([d, d, d], axis=1)


def _hgrn_kernel(q_ref, f_ref, i_ref, g_ref, s0_ref, lbl_ref, gn_ref, dm_ref,
                 y_ref, so_ref, st_s, pad_s, *, t_blk, nh, layer):
    c = HG_CHUNK
    cb = pl.program_id(1)

    @pl.when(cb == 0)
    def _():
        st_s[...] = s0_ref[...]

    lg = lbl_ref[...]
    e = jnp.exp(lg - jnp.max(lg, axis=0, keepdims=True))
    sm = e / jnp.sum(e, axis=0, keepdims=True)
    lb = jnp.sum(sm[1:layer + 1, :], axis=0, keepdims=True)
    log_lb = jnp.log(lb)
    log_1mlb = jnp.log1p(-lb)

    def load(ref, k):
        if t_blk == c:
            return ref[...]
        pad_s[k] = jnp.zeros(pad_s.shape[1:], F32)
        pad_s[k, 0:t_blk, :] = ref[...]
        return pad_s[k]

    q_all, f_all, v_all = load(q_ref, 0), load(f_ref, 1), load(i_ref, 2)

    row = lax.broadcasted_iota(jnp.int32, (c, HD), 0)
    rr = lax.broadcasted_iota(jnp.int32, (c, c), 0)
    cc = lax.broadcasted_iota(jnp.int32, (c, c), 1)
    eye_b = rr == cc
    eye = eye_b.astype(F32)
    live = row < t_blk
    dm = dm_ref[...]

    for h in range(nh):
        sl = slice(h * HD, (h + 1) * HD)
        q, cf, v = q_all[:, sl], f_all[:, sl], v_all[:, sl]
        x1 = log_lb[:, sl]
        x2 = log_1mlb[:, sl] + _log_sigmoid(cf)
        g = jnp.maximum(x1, x2) + jnp.log1p(jnp.exp(-jnp.abs(x1 - x2)))
        kk = (1.0 - lb[:, sl]) * jax.nn.sigmoid(-cf)
        if t_blk != c:
            g = jnp.where(live, g, 0.0)
            kk = jnp.where(live, kk, 0.0)
        hi = g.astype(BF16)
        r1 = g - hi.astype(F32)
        mid = r1.astype(BF16)
        lo = (r1 - mid.astype(F32)).astype(BF16)
        sums = jnp.dot(dm, jnp.concatenate([hi, mid, lo], axis=0), preferred_element_type=F32)
        b = sums[0:c]
        suffix = sums[c:2 * c]
        state = st_s[h]
        vb = v.astype(BF16)
        o = jnp.dot((q * jnp.exp(b)).astype(BF16), state.astype(BF16), preferred_element_type=F32)
        att = jnp.where(eye_b, _dot_nt(q.astype(BF16), kk.astype(BF16)), 0.0)
        for lv in range(1, HG_LEVELS + 1):
            s = 1 << lv
            upper = (row & (s - 1)) >= (s // 2)
            dq = sums[2 * lv * c:(2 * lv + 1) * c]
            dk = sums[(2 * lv + 1) * c:(2 * lv + 2) * c]
            qt = (q * jnp.where(upper, jnp.exp(dq), 0.0)).astype(BF16)
            kt = (kk * jnp.where(upper, 0.0, jnp.exp(dk))).astype(BF16)
            att = att + jnp.where((rr >> lv) == (cc >> lv), _dot_nt(qt, kt), 0.0)
        o = o + jnp.dot(att.astype(BF16), vb, preferred_element_type=F32)
        kd = kk * jnp.exp(suffix)
        ecol = _col_from_row(jnp.exp(b[c - 1:c, :]), eye)
        st_s[h] = ecol * state + jnp.dot(kd.T.astype(BF16), vb, preferred_element_type=F32)
        gate = g_ref[:, sl]
        oc = _rms(o[0:t_blk], gn_ref[...]) * (gate * jax.nn.sigmoid(gate))
        y_ref[:, sl] = oc

    so_ref[...] = st_s[...]


def hgrn_mixer(z, s0, lb_logits, g_norm, *, n, t, row0, layer):
    nh = s0.shape[1]
    w = nh * HD
    t_blk = min(t, HG_CHUNK)
    nc = t // t_blk
    rb0 = row0 // t_blk
    dm = jnp.asarray(_hgrn_sum_matrix(), BF16)

    def col(k):
        return pl.BlockSpec((t_blk, w), lambda i, b: (rb0 + i * nc + b, k))

    return pl.pallas_call(
        functools.partial(_hgrn_kernel, t_blk=t_blk, nh=nh, layer=layer),
        grid=(n, nc),
        in_specs=[
            col(0), col(1), col(2), col(3),
            pl.BlockSpec((None, nh, HD, HD), lambda i, b: (i, 0, 0, 0)),
            pl.BlockSpec(lb_logits.shape, lambda i, b: (0, 0)),
            pl.BlockSpec((1, HD), lambda i, b: (0, 0)),
            pl.BlockSpec(dm.shape, lambda i, b: (0, 0)),
        ],
        out_specs=[
            pl.BlockSpec((t_blk, w), lambda i, b: (i * nc + b, 0)),
            pl.BlockSpec((None, nh, HD, HD), lambda i, b: (i, 0, 0, 0)),
        ],
        out_shape=[
            jax.ShapeDtypeStruct((n * t, w), F32),
            jax.ShapeDtypeStruct((n, nh, HD, HD), F32),
        ],
        scratch_shapes=[
            pltpu.VMEM((nh, HD, HD), F32),
            pltpu.VMEM((3, HG_CHUNK, w), F32),
        ],
        compiler_params=_params("parallel", "arbitrary"),
        name="hgrn_mixer",
    )(z, z, z, z, s0, lb_logits, g_norm.reshape(1, HD), dm)


def _merge_groups(outs, lses):
    mx = functools.reduce(jnp.maximum, lses)
    ws = [jnp.exp(l - mx) for l in lses]
    den = functools.reduce(lambda a, b: a + b, ws)
    return functools.reduce(lambda a, b: a + b, [w * o for w, o in zip(ws, outs)]) / den


def _dsw_prompt_kernel(q0_ref, q1_ref, q2_ref, k_ref, v_ref, o_ref, m_s, l_s, acc_s, *, tq, nkv, nblk, scale):
    qrefs = (q0_ref, q1_ref, q2_ref)
    qi = pl.program_id(1)
    kb = pl.program_id(2)
    nkb = pl.num_programs(2)

    @pl.when(kb == 0)
    def _():
        m_s[...] = jnp.full_like(m_s, -jnp.inf)
        l_s[...] = jnp.zeros_like(l_s)
        acc_s[...] = jnp.zeros_like(acc_s)

    @pl.when(kb <= qi)
    def _():
        dist = (kb * tq + lax.broadcasted_iota(jnp.int32, (tq, tq), 0)
                - lax.broadcasted_iota(jnp.int32, (tq, tq), 1))
        for g, (win, dil) in enumerate(D_GROUPS):
            @pl.when(kb < nblk[g])
            def _():
                valid = (dist >= 0) & (dist <= win) & ((dist & (dil - 1)) == 0)
                for h in range(nkv):
                    sl = slice(h * HD, (h + 1) * HD)
                    idx = g * nkv + h
                    s = _dot_nt(qrefs[g][:, sl].astype(BF16), k_ref[:, sl].astype(BF16)) * scale
                    s = jnp.where(valid, s, -jnp.inf)
                    m_prev = m_s[idx]
                    m_new = jnp.maximum(m_prev, jnp.max(s, axis=-1, keepdims=True))
                    alpha = jnp.exp(m_prev - m_new)
                    p = jnp.exp(s - m_new)
                    l_s[idx] = alpha * l_s[idx] + jnp.sum(p, axis=-1, keepdims=True)
                    acc_s[idx] = alpha * acc_s[idx] + jnp.dot(
                        p.astype(BF16), v_ref[:, sl].astype(BF16), preferred_element_type=F32)
                    m_s[idx] = m_new

    @pl.when(kb == nkb - 1)
    def _():
        for h in range(nkv):
            idxs = [g * nkv + h for g in range(len(D_GROUPS))]
            outs = [acc_s[i] / l_s[i] for i in idxs]
            lses = [m_s[i] + jnp.log(l_s[i]) for i in idxs]
            o_ref[:, h * HD:(h + 1) * HD] = _merge_groups(outs, lses)


def dsw_prompt(z, *, n, s_len, tq, nkv, qcol):
    w = nkv * HD
    nq = s_len // tq
    ng = len(D_GROUPS)
    nblk = tuple(-(-win // tq) + 1 for win, _ in D_GROUPS)
    nkb = min(max(nblk), nq)
    qspec = lambda g: pl.BlockSpec((tq, w), lambda i, a, b: (i * nq + a, qcol + g))
    kv = lambda k: pl.BlockSpec((tq, w), lambda i, a, b: (i * nq + jnp.maximum(a - b, 0), qcol + ng + k))
    return pl.pallas_call(
        functools.partial(_dsw_prompt_kernel, tq=tq, nkv=nkv, nblk=nblk, scale=HD ** -0.5),
        grid=(n, nq, nkb),
        in_specs=[qspec(0), qspec(1), qspec(2), kv(0), kv(1)],
        out_specs=pl.BlockSpec((tq, w), lambda i, a, b: (i * nq + a, 0)),
        out_shape=jax.ShapeDtypeStruct((n * s_len, w), F32),
        scratch_shapes=[
            pltpu.VMEM((ng * nkv, tq, 1), F32),
            pltpu.VMEM((ng * nkv, tq, 1), F32),
            pltpu.VMEM((ng * nkv, tq, HD), F32),
        ],
        compiler_params=_params("parallel", "parallel", "arbitrary"),
        name="dsw_prompt",
    )(z, z, z, z, z)


def _dsw_sample_kernel(q0_ref, q1_ref, q2_ref, kn_ref, vn_ref, kc_ref, vc_ref, o_ref, kn_s, vn_s,
                       *, t, nkv, wbuf, scale):
    ng = len(D_GROUPS)
    kn_s[...] = jnp.zeros_like(kn_s)
    vn_s[...] = jnp.zeros_like(vn_s)
    kn_s[0:t, :] = kn_ref[...]
    vn_s[0:t, :] = vn_ref[...]

    def group_consts(shape):
        r = lax.broadcasted_iota(jnp.int32, shape, 0)
        grp = r // t
        win = jnp.zeros(shape, jnp.int32)
        dmask = jnp.zeros(shape, jnp.int32)
        for g, (w_, d_) in enumerate(D_GROUPS):
            win = jnp.where(grp == g, w_, win)
            dmask = jnp.where(grp == g, d_ - 1, dmask)
        return r % t, win, dmask

    tok_c, win_c, dm_c = group_consts((ng * t, wbuf))
    dist_c = wbuf + tok_c - lax.broadcasted_iota(jnp.int32, (ng * t, wbuf), 1)
    valid_c = (dist_c >= 0) & (dist_c <= win_c) & ((dist_c & dm_c) == 0)
    tok_n, win_n, dm_n = group_consts((ng * t, HD))
    dist_n = tok_n - lax.broadcasted_iota(jnp.int32, (ng * t, HD), 1)
    valid_n = (dist_n >= 0) & (dist_n <= win_n) & ((dist_n & dm_n) == 0)

    for h in range(nkv):
        sl = slice(h * HD, (h + 1) * HD)
        q = jnp.concatenate([q0_ref[:, sl], q1_ref[:, sl], q2_ref[:, sl]], axis=0).astype(BF16)
        s_c = jnp.where(valid_c, _dot_nt(q, kc_ref[:, sl].astype(BF16)) * scale, -jnp.inf)
        s_n = jnp.where(valid_n, _dot_nt(q, kn_s[:, sl].astype(BF16)) * scale, -jnp.inf)
        m = jnp.maximum(jnp.max(s_c, axis=-1, keepdims=True), jnp.max(s_n, axis=-1, keepdims=True))
        p_c = jnp.exp(s_c - m)
        p_n = jnp.exp(s_n - m)
        l = jnp.sum(p_c, axis=-1, keepdims=True) + jnp.sum(p_n, axis=-1, keepdims=True)
        o = jnp.dot(p_c.astype(BF16), vc_ref[:, sl].astype(BF16), preferred_element_type=F32)
        o = (o + jnp.dot(p_n.astype(BF16), vn_s[:, sl].astype(BF16), preferred_element_type=F32)) / l
        lse = m + jnp.log(l)
        o_ref[:, sl] = _merge_groups([o[g * t:(g + 1) * t] for g in range(ng)],
                                     [lse[g * t:(g + 1) * t] for g in range(ng)])


def dsw_sample(z, cache_k, cache_v, *, n, t, nkv, qcol, row0):
    w = nkv * HD
    wbuf = cache_k.shape[1]
    ng = len(D_GROUPS)
    rb0 = row0 // t
    col = lambda k: pl.BlockSpec((t, w), lambda i: (rb0 + i, qcol + k))
    cache = pl.BlockSpec((None, wbuf, w), lambda i: (i, 0, 0))
    return pl.pallas_call(
        functools.partial(_dsw_sample_kernel, t=t, nkv=nkv, wbuf=wbuf, scale=HD ** -0.5),
        grid=(n,),
        in_specs=[col(0), col(1), col(2), col(ng), col(ng + 1), cache, cache],
        out_specs=pl.BlockSpec((t, w), lambda i: (i, 0)),
        out_shape=jax.ShapeDtypeStruct((n * t, w), F32),
        scratch_shapes=[pltpu.VMEM((HD, w), F32), pltpu.VMEM((HD, w), F32)],
        compiler_params=_params("parallel"),
        name="dsw_sample",
    )(z, z, z, z, z, cache_k, cache_v)


def kernel(x_prompt, x_sample, cache_fox_k, cache_fox_v, cache_fox_logf, page_table, state_lru_conv, state_lru_h, state_hgrn, cache_dsw_k, cache_dsw_v, norm_gains, ffn_w_gate, ffn_w_up, ffn_w_down, even_w_in, even_w_out, lru_conv_w, lru_conv_b, lru_w_r, lru_b_r, lru_w_i, lru_b_i, lru_lambda, fox_b_f, odd_w_in, odd_w_out, hgrn_lb_logits, hgrn_norm, final_norm):
    n_p, s_len, d = x_prompt.shape
    n_s, t_s, _ = x_sample.shape
    depth = norm_gains.shape[0]
    a_w = lru_lambda.shape[-1]
    b_heads = fox_b_f.shape[-1]
    b_w = b_heads * HD
    c_heads = state_hgrn.shape[2]
    c_w = c_heads * HD
    d_heads = cache_dsw_k.shape[3]
    d_w = d_heads * HD
    n_pool = cache_fox_k.shape[1]
    wbuf = cache_dsw_k.shape[2]
    keep = min(max(w for w, _ in D_GROUPS), s_len)

    xp = x_prompt.reshape(n_p * s_len, d)
    xs = x_sample.reshape(n_s * t_s, d)
    tm_p, tm_s = 512, n_s * t_s
    names = ('fox_k', 'fox_v', 'fox_logf', 'lru_conv', 'lru_h', 'hgrn', 'dsw_k', 'dsw_v')
    st_p = {nm: [] for nm in names}
    st_s = {nm: [] for nm in names}

    def ffn_both(xp, xs, l, k, final):
        wg, wu, wd = (a[l, k].astype(BF16) for a in (ffn_w_gate, ffn_w_up, ffn_w_down))
        g = norm_gains[l, 2 * k]
        return (ffn(xp, g, wg, wu, wd, final_norm, final=final, tm=tm_p, tf=512),
                ffn(xs, g, wg, wu, wd, final_norm, final=final, tm=tm_s, tf=512))

    for l in range(depth):
        j = l // 2
        xp, xs = ffn_both(xp, xs, l, 0, False)
        g_mix = norm_gains[l, 1]
        if l % 2 == 0:
            n_main = 2 * a_w + 3 * b_w
            w_main = even_w_in[j][:, :n_main].astype(BF16)
            w_f = jnp.pad(even_w_in[j][:, n_main:], ((0, 0), (0, HD - b_heads))).astype(BF16)
            bfp = jnp.pad(fox_b_f[j], (0, HD - b_heads)).reshape(1, HD)
            wri = jnp.concatenate([lru_w_r[j], lru_w_i[j]], axis=-1).astype(BF16)
            w_out = even_w_out[j].astype(BF16)
            lru_args = (lru_conv_w[j], lru_conv_b[j], wri, lru_b_r[j], lru_b_i[j], lru_lambda[j])
            qcol = 2 * a_w // b_w

            zp = norm_matmul(xp, g_mix, w_main, tm=tm_p, tn=512)
            zfp = norm_matmul(xp, g_mix, w_f, tm=tm_p, tn=HD)
            zs = norm_matmul(xs, g_mix, w_main, tm=tm_s, tn=512)
            zfs = norm_matmul(xs, g_mix, w_f, tm=tm_s, tn=HD)

            ya_p, cv_p, h_p = lru_mixer(zp, jnp.zeros((n_p, CONV_W - 1, a_w), F32), jnp.zeros((n_p, a_w), F32),
                                        *lru_args, n=n_p, t=s_len, tt=256, starts=True)
            lf_p, cum_p, cumt_p = fox_prep(zfp, bfp, n=n_p, s_len=s_len, nh=b_heads)
            ob_p = fox_prompt(zp, cum_p, cumt_p, n=n_p, s_len=s_len, tq=512, qcol=qcol)

            ya_s, cv_s, h_s = lru_mixer(zs, state_lru_conv[j], state_lru_h[j],
                                        *lru_args, n=n_s, t=t_s, tt=t_s, starts=False)
            lf_s, _, _ = fox_prep(zfs, bfp, n=1, s_len=n_s * t_s, nh=b_heads)
            k_s = zs[:, 2 * a_w + b_w:2 * a_w + 2 * b_w]
            v_s = zs[:, 2 * a_w + 2 * b_w:2 * a_w + 3 * b_w]
            pad_rows = lambda a: jnp.pad(a.reshape(n_s, t_s, -1), ((0, 0), (0, HD - t_s), (0, 0)))
            ob_s = fox_sample(page_table, zs, pad_rows(k_s), pad_rows(v_s), pad_rows(lf_s),
                              cache_fox_k[j].reshape(n_pool, HD, b_w), cache_fox_v[j].reshape(n_pool, HD, b_w),
                              cache_fox_logf[j], n=n_s, t=t_s, nh=b_heads, qcol=qcol, pp=4)

            xp = out_proj(xp, ya_p, ob_p, w_out[:a_w], w_out[a_w:], tm=tm_p, tn=512)
            xs = out_proj(xs, ya_s, ob_s, w_out[:a_w], w_out[a_w:], tm=tm_s, tn=512)

            k_p = zp[:, 2 * a_w + b_w:2 * a_w + 2 * b_w]
            v_p = zp[:, 2 * a_w + 2 * b_w:2 * a_w + 3 * b_w]
            st_p['fox_k'].append(k_p.reshape(n_p, s_len, b_heads, HD))
            st_p['fox_v'].append(v_p.reshape(n_p, s_len, b_heads, HD))
            st_p['fox_logf'].append(lf_p.reshape(n_p, s_len, b_heads))
            st_p['lru_conv'].append(cv_p)
            st_p['lru_h'].append(h_p.reshape(n_p, a_w))
            st_s['fox_k'].append(k_s.reshape(n_s, t_s, b_heads, HD))
            st_s['fox_v'].append(v_s.reshape(n_s, t_s, b_heads, HD))
            st_s['fox_logf'].append(lf_s.reshape(n_s, t_s, b_heads))
            st_s['lru_conv'].append(cv_s)
            st_s['lru_h'].append(h_s.reshape(n_s, a_w))
        else:
            w_in = odd_w_in[j].astype(BF16)
            w_out = odd_w_out[j].astype(BF16)
            qcol = 4 * c_w // d_w
            zp = norm_matmul(xp, g_mix, w_in, tm=tm_p, tn=512)
            zs = norm_matmul(xs, g_mix, w_in, tm=tm_s, tn=512)

            oc_p, s_p = hgrn_mixer(zp, jnp.zeros((n_p, c_heads, HD, HD), F32), hgrn_lb_logits, hgrn_norm[j],
                                   n=n_p, t=s_len, row0=0, layer=l)
            od_p = dsw_prompt(zp, n=n_p, s_len=s_len, tq=512, nkv=d_heads, qcol=qcol)
            oc_s, s_s = hgrn_mixer(zs, state_hgrn[j], hgrn_lb_logits, hgrn_norm[j],
                                   n=n_s, t=t_s, row0=0, layer=l)
            od_s = dsw_sample(zs, cache_dsw_k[j].reshape(n_s, wbuf, d_w), cache_dsw_v[j].reshape(n_s, wbuf, d_w),
                              n=n_s, t=t_s, nkv=d_heads, qcol=qcol, row0=0)

            xp = out_proj(xp, oc_p, od_p, w_out[:c_w], w_out[c_w:], tm=tm_p, tn=512)
            xs = out_proj(xs, oc_s, od_s, w_out[:c_w], w_out[c_w:], tm=tm_s, tn=512)

            k0 = 4 * c_w + len(D_GROUPS) * d_w
            zp3 = zp.reshape(n_p, s_len, -1)
            st_p['hgrn'].append(s_p)
            st_p['dsw_k'].append(zp3[:, s_len - keep:, k0:k0 + d_w].reshape(n_p, keep, d_heads, HD))
            st_p['dsw_v'].append(zp3[:, s_len - keep:, k0 + d_w:k0 + 2 * d_w].reshape(n_p, keep, d_heads, HD))
            st_s['hgrn'].append(s_s)
            st_s['dsw_k'].append(zs[:, k0:k0 + d_w].reshape(n_s, t_s, d_heads, HD))
            st_s['dsw_v'].append(zs[:, k0 + d_w:k0 + 2 * d_w].reshape(n_s, t_s, d_heads, HD))
        xp, xs = ffn_both(xp, xs, l, 1, l == depth - 1)

    y_prompt = xp.reshape(n_p, s_len, d)
    y_sample = xs.reshape(n_s, t_s, d)
    return (y_prompt, y_sample,
            *(jnp.stack(st_p[nm]) for nm in names),
            *(jnp.stack(st_s[nm]) for nm in names))
```

```python
import functools

import numpy as np
import jax
import jax.numpy as jnp
from jax import lax
from jax.experimental import pallas as pl
from jax.experimental.pallas import tpu as pltpu

F32 = jnp.float32
BF16 = jnp.bfloat16
EPS = 1e-6
HD = 128
LRU_C = 8.0
CONV_W = 4
D_GROUPS = ((128, 1), (512, 4), (2048, 16))
VMEM_LIMIT_BYTES = 56 * 1024 * 1024
HIGHEST = lax.Precision.HIGHEST
NT_DIMS = (((1,), (1,)), ((), ()))


def _params(*sem):
    return pltpu.CompilerParams(dimension_semantics=sem, vmem_limit_bytes=VMEM_LIMIT_BYTES)


def _rms(x, g):
    ms = jnp.mean(x * x, axis=-1, keepdims=True)
    return x * lax.rsqrt(ms + EPS) * g


def _softplus(z):
    return jnp.maximum(z, 0.0) + jnp.log1p(jnp.exp(-jnp.abs(z)))


def _log_sigmoid(z):
    return -_softplus(-z)


def _dot_nt(a, b):
    return lax.dot_general(a, b, NT_DIMS, preferred_element_type=F32)


def _col_from_row(row, eye):
    return jnp.sum(eye * row, axis=1, keepdims=True)


def _ffn_kernel(x_ref, g_ref, wg_ref, wu_ref, wd_ref, gf_ref, o_ref, xn_ref, *, nf, final):
    f = pl.program_id(1)

    @pl.when(f == 0)
    def _():
        xn_ref[...] = _rms(x_ref[...], g_ref[...]).astype(BF16)
        o_ref[...] = jnp.zeros_like(o_ref)

    xn = xn_ref[...]
    a = jnp.dot(xn, wg_ref[...].astype(BF16), preferred_element_type=F32)
    b = jnp.dot(xn, wu_ref[...].astype(BF16), preferred_element_type=F32)
    h = (a * jax.nn.sigmoid(a) * b).astype(BF16)
    o_ref[...] += jnp.dot(h, wd_ref[...].astype(BF16), preferred_element_type=F32)

    @pl.when(f == nf - 1)
    def _():
        y = x_ref[...] + 0.5 * o_ref[...]
        if final:
            y = _rms(y, gf_ref[...])
        o_ref[...] = y


def ffn(x, g, wg, wu, wd, gf, *, l, k, final, tm, tf):
    m, d = x.shape
    nf = wg.shape[-1] // tf
    return pl.pallas_call(
        functools.partial(_ffn_kernel, nf=nf, final=final),
        grid=(m // tm, nf),
        in_specs=[
            pl.BlockSpec((tm, d), lambda i, f: (i, 0), pipeline_mode=pl.Buffered(1)),
            pl.BlockSpec((1, d), lambda i, f: (0, 0)),
            pl.BlockSpec((None, None, d, tf), lambda i, f: (l, k, 0, f)),
            pl.BlockSpec((None, None, d, tf), lambda i, f: (l, k, 0, f)),
            pl.BlockSpec((None, None, tf, d), lambda i, f: (l, k, f, 0)),
            pl.BlockSpec((1, d), lambda i, f: (0, 0)),
        ],
        out_specs=pl.BlockSpec((tm, d), lambda i, f: (i, 0)),
        out_shape=jax.ShapeDtypeStruct((m, d), F32),
        scratch_shapes=[pltpu.VMEM((tm, d), BF16)],
        compiler_params=_params("parallel", "arbitrary"),
        name="ffn",
    )(x, g.reshape(1, d), wg, wu, wd, gf.reshape(1, d))


def _norm_matmul_kernel(x_ref, g_ref, w_ref, o_ref, xn_ref):
    @pl.when(pl.program_id(1) == 0)
    def _():
        xn_ref[...] = _rms(x_ref[...], g_ref[...]).astype(BF16)

    o_ref[...] = jnp.dot(xn_ref[...], w_ref[...].astype(BF16), preferred_element_type=F32)


def norm_matmul(x, g, w, *, tm, tn):
    m, d = x.shape
    n = w.shape[1] // tn * tn
    return pl.pallas_call(
        _norm_matmul_kernel,
        grid=(m // tm, n // tn),
        in_specs=[
            pl.BlockSpec((tm, d), lambda i, j: (i, 0), pipeline_mode=pl.Buffered(1)),
            pl.BlockSpec((1, d), lambda i, j: (0, 0)),
            pl.BlockSpec((d, tn), lambda i, j: (0, j)),
        ],
        out_specs=pl.BlockSpec((tm, tn), lambda i, j: (i, j)),
        out_shape=jax.ShapeDtypeStruct((m, n), F32),
        scratch_shapes=[pltpu.VMEM((tm, d), BF16)],
        compiler_params=_params("parallel", "arbitrary"),
        name="norm_matmul",
    )(x, g.reshape(1, d), w)


def _out_proj_kernel(r_ref, ya_ref, yb_ref, wa_ref, wb_ref, o_ref):
    acc = jnp.dot(ya_ref[...].astype(BF16), wa_ref[...].astype(BF16), preferred_element_type=F32)
    acc = acc + jnp.dot(yb_ref[...].astype(BF16), wb_ref[...].astype(BF16), preferred_element_type=F32)
    o_ref[...] = r_ref[...] + acc


def out_proj(res, ya, yb, w, *, tm, tn):
    m, d = res.shape
    ka, kb = ya.shape[1], yb.shape[1]
    assert ka % kb == 0
    return pl.pallas_call(
        _out_proj_kernel,
        grid=(m // tm, d // tn),
        in_specs=[
            pl.BlockSpec((tm, tn), lambda i, j: (i, j)),
            pl.BlockSpec((tm, ka), lambda i, j: (i, 0)),
            pl.BlockSpec((tm, kb), lambda i, j: (i, 0)),
            pl.BlockSpec((ka, tn), lambda i, j: (0, j)),
            pl.BlockSpec((kb, tn), lambda i, j: (ka // kb, j)),
        ],
        out_specs=pl.BlockSpec((tm, tn), lambda i, j: (i, j)),
        out_shape=jax.ShapeDtypeStruct((m, d), F32),
        compiler_params=_params("parallel", "arbitrary"),
        name="out_proj",
    )(res, ya, yb, w, w)


def _lru_kernel(ax_ref, ag_ref, cb0_ref, h0_ref, cw_ref, cb_ref, wri_ref, br_ref, bi_ref, lam_ref,
                ya_ref, cst_ref, hl_ref, xbuf, hcar, a_s, b_s, *, tt, nh, starts):
    tb = pl.program_id(1)

    @pl.when(tb == 0)
    def _():
        xbuf[5:8, :] = cb0_ref[...]
        hcar[...] = h0_ref[...]

    x = ax_ref[...]
    xbuf[8:8 + tt, :] = x
    cw = cw_ref[...]
    y = cb_ref[...] + xbuf[5:5 + tt, :] * cw[0:1, :]
    y = y + xbuf[6:6 + tt, :] * cw[1:2, :]
    y = y + xbuf[7:7 + tt, :] * cw[2:3, :]
    y = y + x * cw[3:4, :]
    tail = xbuf[5 + tt:8 + tt, :]
    xbuf[5:8, :] = tail
    cst_ref[...] = tail

    sp = _softplus(-lam_ref[...])
    for h in range(nh):
        sl = slice(h * HD, (h + 1) * HD)
        yh = y[:, sl]
        gts = jnp.dot(yh.astype(BF16), wri_ref[h], preferred_element_type=F32)
        gr = jax.nn.sigmoid(gts[:, :HD] + br_ref[:, sl])
        gi = jax.nn.sigmoid(gts[:, HD:] + bi_ref[:, sl])
        log_a = -LRU_C * gr * sp[:, sl]
        a = jnp.exp(log_a)
        mult = jnp.sqrt(-jnp.tanh(log_a) * (a * a + 1.0))
        if starts:
            row = lax.broadcasted_iota(jnp.int32, mult.shape, 0)
            mult = jnp.where((row == 0) & (tb == 0), 1.0, mult)
        a_s[:, sl] = a
        b_s[:, sl] = mult * gi * yh

    def group(gidx, hprev):
        r = pl.multiple_of(gidx * 8, 8)
        a8 = a_s[pl.ds(r, 8), :]
        b8 = b_s[pl.ds(r, 8), :]
        row = lax.broadcasted_iota(jnp.int32, a8.shape, 0)
        for s in (1, 2, 4):
            ok = row >= s
            b8 = jnp.where(ok, a8 * pltpu.roll(b8, s, 0) + b8, b8)
            a8 = jnp.where(ok, a8 * pltpu.roll(a8, s, 0), a8)
        h8 = a8 * hprev + b8
        b_s[pl.ds(r, 8), :] = h8
        return h8[7:8, :]

    hlast = lax.fori_loop(0, tt // 8, group, hcar[...])
    hcar[...] = hlast
    hl_ref[...] = hlast
    ya_ref[...] = b_s[...] * jax.nn.gelu(ag_ref[...])


def lru_mixer(z, conv0, h0, cw, cb, wri, br, bi, lam, *, n, t, tt, starts):
    w = lam.shape[-1]
    nh = w // HD
    nb = t // tt
    row = lambda i, b: (i * nb + b, 0)
    vec = pl.BlockSpec((1, w), lambda i, b: (0, 0))
    return pl.pallas_call(
        functools.partial(_lru_kernel, tt=tt, nh=nh, starts=starts),
        grid=(n, nb),
        in_specs=[
            pl.BlockSpec((tt, w), row),
            pl.BlockSpec((tt, w), lambda i, b: (i * nb + b, 1)),
            pl.BlockSpec((None, CONV_W - 1, w), lambda i, b: (i, 0, 0)),
            pl.BlockSpec((None, 1, w), lambda i, b: (i, 0, 0)),
            pl.BlockSpec((CONV_W, w), lambda i, b: (0, 0)),
            vec,
            pl.BlockSpec((nh, HD, 2 * HD), lambda i, b: (0, 0, 0)),
            vec, vec, vec,
        ],
        out_specs=[
            pl.BlockSpec((tt, w), row),
            pl.BlockSpec((None, CONV_W - 1, w), lambda i, b: (i, 0, 0)),
            pl.BlockSpec((None, 1, w), lambda i, b: (i, 0, 0)),
        ],
        out_shape=[
            jax.ShapeDtypeStruct((n * t, w), F32),
            jax.ShapeDtypeStruct((n, CONV_W - 1, w), F32),
            jax.ShapeDtypeStruct((n, 1, w), F32),
        ],
        scratch_shapes=[
            pltpu.VMEM((tt + 8, w), F32),
            pltpu.VMEM((1, w), F32),
            pltpu.VMEM((tt, w), F32),
            pltpu.VMEM((tt, w), F32),
        ],
        compiler_params=_params("parallel", "arbitrary"),
        name="lru_mixer",
    )(z, z, conv0, h0.reshape(n, 1, w), cw, cb.reshape(1, w), wri,
      br.reshape(1, w), bi.reshape(1, w), lam.reshape(1, w))


def _fox_prep_kernel(zf_ref, bf_ref, lf_ref, cum_ref, cumt_ref, cum_s, cumt_s, *, s_len, nh):
    lf = _log_sigmoid(zf_ref[...] + bf_ref[...])
    lf_ref[...] = lf[:, :nh]
    cum_s[...] = lf
    r = lax.broadcasted_iota(jnp.int32, (HD, HD), 0)
    c = lax.broadcasted_iota(jnp.int32, (HD, HD), 1)
    tri = (r >= c).astype(F32)

    def chunk(i, carry):
        st = pl.multiple_of(i * HD, HD)
        cs = jnp.dot(tri, cum_s[pl.ds(st, HD), :], preferred_element_type=F32, precision=HIGHEST) + carry
        cum_s[pl.ds(st, HD), :] = cs
        cumt_s[:, pl.ds(st, HD)] = cs.T
        return cs[HD - 1:HD, :]

    lax.fori_loop(0, s_len // HD, chunk, jnp.zeros((1, HD), F32))
    cum_ref[...] = cum_s[:, :nh]
    cumt_ref[...] = cumt_s[:nh, :]


def fox_prep(zf, bfp, *, n, s_len, nh):
    return pl.pallas_call(
        functools.partial(_fox_prep_kernel, s_len=s_len, nh=nh),
        grid=(n,),
        in_specs=[
            pl.BlockSpec((s_len, HD), lambda i: (i, 0)),
            pl.BlockSpec((1, HD), lambda i: (0, 0)),
        ],
        out_specs=[
            pl.BlockSpec((s_len, nh), lambda i: (i, 0)),
            pl.BlockSpec((s_len, nh), lambda i: (i, 0)),
            pl.BlockSpec((None, nh, s_len), lambda i: (i, 0, 0)),
        ],
        out_shape=[
            jax.ShapeDtypeStruct((n * s_len, nh), F32),
            jax.ShapeDtypeStruct((n * s_len, nh), F32),
            jax.ShapeDtypeStruct((n, nh, s_len), F32),
        ],
        scratch_shapes=[pltpu.VMEM((s_len, HD), F32), pltpu.VMEM((HD, s_len), F32)],
        compiler_params=_params("parallel"),
        name="fox_prep",
    )(zf, bfp)


def _fox_prompt_kernel(q_ref, k_ref, v_ref, cq_ref, ck_ref, o_ref, m_s, l_s, acc_s, *, tq, nh, scale):
    qi = pl.program_id(1)
    kj = pl.program_id(2)

    @pl.when(kj == 0)
    def _():
        m_s[...] = jnp.full_like(m_s, -jnp.inf)
        l_s[...] = jnp.zeros_like(l_s)
        acc_s[...] = jnp.zeros_like(acc_s)

    def block(causal):
        for h in range(nh):
            sl = slice(h * HD, (h + 1) * HD)
            s = _dot_nt(q_ref[:, sl].astype(BF16), k_ref[:, sl].astype(BF16)) * scale
            s = s + (cq_ref[:, h:h + 1] - ck_ref[h:h + 1, :])
            if causal is not None:
                s = jnp.where(causal, s, -jnp.inf)
            m_prev = m_s[h]
            m_new = jnp.maximum(m_prev, jnp.max(s, axis=-1, keepdims=True))
            alpha = jnp.exp(m_prev - m_new)
            p = jnp.exp(s - m_new)
            l_s[h] = alpha * l_s[h] + jnp.sum(p, axis=-1, keepdims=True)
            acc_s[:, sl] = alpha * acc_s[:, sl] + jnp.dot(
                p.astype(BF16), v_ref[:, sl].astype(BF16), preferred_element_type=F32)
            m_s[h] = m_new

    @pl.when(kj < qi)
    def _():
        block(None)

    @pl.when(kj == qi)
    def _():
        row = lax.broadcasted_iota(jnp.int32, (tq, tq), 0)
        col = lax.broadcasted_iota(jnp.int32, (tq, tq), 1)
        block(col <= row)
        for h in range(nh):
            sl = slice(h * HD, (h + 1) * HD)
            o_ref[:, sl] = acc_s[:, sl] / l_s[h]


def fox_prompt(z, cum, cumt, *, n, s_len, tq, qcol):
    nh = cum.shape[1]
    w = nh * HD
    nq = s_len // tq
    return pl.pallas_call(
        functools.partial(_fox_prompt_kernel, tq=tq, nh=nh, scale=HD ** -0.5),
        grid=(n, nq, nq),
        in_specs=[
            pl.BlockSpec((tq, w), lambda i, a, b: (i * nq + a, qcol)),
            pl.BlockSpec((tq, w), lambda i, a, b: (i * nq + jnp.minimum(a, b), qcol + 1)),
            pl.BlockSpec((tq, w), lambda i, a, b: (i * nq + jnp.minimum(a, b), qcol + 2)),
            pl.BlockSpec((tq, nh), lambda i, a, b: (i * nq + a, 0)),
            pl.BlockSpec((None, nh, tq), lambda i, a, b: (i, 0, jnp.minimum(a, b))),
        ],
        out_specs=pl.BlockSpec((tq, w), lambda i, a, b: (i * nq + a, 0)),
        out_shape=jax.ShapeDtypeStruct((n * s_len, w), F32),
        scratch_shapes=[
            pltpu.VMEM((nh, tq, 1), F32),
            pltpu.VMEM((nh, tq, 1), F32),
            pltpu.VMEM((tq, w), F32),
        ],
        compiler_params=_params("parallel", "parallel", "arbitrary"),
        name="fox_prompt",
    )(z, z, z, cum, cumt)


def _split3(x):
    hi = x.astype(BF16)
    r1 = x - hi.astype(F32)
    mid = r1.astype(BF16)
    return hi, mid, (r1 - mid.astype(F32)).astype(BF16)


def _fox_sample_kernel(pt_ref, q_ref, kn_ref, vn_ref, lfn_ref, later_ref, *refs, pp, t, nh, scale):
    kc, vc, lc = refs[0:pp], refs[pp:2 * pp], refs[2 * pp:3 * pp]
    o_ref = refs[3 * pp]
    qb_s, crow_s, carry_s, m_s, l_s, acc_s = refs[3 * pp + 1:]
    j = pl.program_id(1)
    nj = pl.num_programs(1)
    ncol = nh * t

    r128 = lax.broadcasted_iota(jnp.int32, (HD, HD), 0)
    c128 = lax.broadcasted_iota(jnp.int32, (HD, HD), 1)
    eye = (r128 == c128).astype(F32)
    hrow = lax.broadcasted_iota(jnp.int32, (nh, HD), 0)
    hcol = lax.broadcasted_iota(jnp.int32, (nh, HD), 1)
    expand = ((hcol // t == hrow) & (hcol < ncol)).astype(BF16)

    def to_columns(lf):
        return [jnp.dot(part, expand, preferred_element_type=F32).astype(BF16) for part in _split3(lf)]

    def block(k2d, v2d, lf, valid):
        rows = k2d.shape[0]
        parts = to_columns(lf)
        later_in_block = later_ref[0:rows, 0:rows]
        later = carry_s[...] + crow_s[...]
        total = jnp.zeros((1, HD), F32)
        for part in parts:
            later = later + jnp.dot(later_in_block, part, preferred_element_type=F32)
            total = total + jnp.sum(part.astype(F32), axis=0, keepdims=True)
        carry_s[...] = carry_s[...] + total
        s = _dot_nt(k2d, qb_s[...]) * scale + later
        if valid is not None:
            s = jnp.where(valid, s, -jnp.inf)
        m_prev = m_s[...]
        m_new = jnp.maximum(m_prev, jnp.max(s, axis=0, keepdims=True))
        alpha = jnp.exp(m_prev - m_new)
        p = jnp.exp(s - m_new)
        l_s[...] = alpha * l_s[...] + jnp.sum(p, axis=0, keepdims=True)
        m_s[...] = m_new
        acc_s[...] = _col_from_row(alpha, eye) * acc_s[...] + jnp.dot(
            p.T.astype(BF16), v2d, preferred_element_type=F32)

    def heads_to_lanes(ref):
        return jnp.concatenate([ref[:, h, :] for h in range(nh)], axis=1).astype(BF16)

    @pl.when(j == 0)
    def _():
        q = q_ref[...]
        w = q.shape[1]
        qrep = jnp.concatenate([q] * nh + [jnp.zeros((HD - ncol, w), F32)], axis=0)
        rr = lax.broadcasted_iota(jnp.int32, (HD, w), 0)
        cc = lax.broadcasted_iota(jnp.int32, (HD, w), 1)
        qb_s[...] = jnp.where(rr // t == cc // HD, qrep, 0.0).astype(BF16)
        m_s[...] = jnp.full_like(m_s, -jnp.inf)
        l_s[...] = jnp.zeros_like(l_s)
        acc_s[...] = jnp.zeros_like(acc_s)
        carry_s[...] = jnp.zeros_like(carry_s)
        lfn = lfn_ref[...]
        lfn_cols = functools.reduce(lambda a, b: a + b, [p.astype(F32) for p in to_columns(lfn)])
        after = r128 > (c128 % t)
        crow_s[...] = -jnp.sum(jnp.where(after, lfn_cols, 0.0), axis=0, keepdims=True)
        block(kn_ref[...].astype(BF16), vn_ref[...].astype(BF16), lfn, r128 <= (c128 % t))

    @pl.when(j > 0)
    def _():
        block(jnp.concatenate([heads_to_lanes(r) for r in kc], axis=0),
              jnp.concatenate([heads_to_lanes(r) for r in vc], axis=0),
              jnp.concatenate([r[...] for r in lc], axis=0), None)

    @pl.when(j == nj - 1)
    def _():
        lcol = _col_from_row(l_s[...], eye)
        for h in range(nh):
            o_ref[:, h * HD:(h + 1) * HD] = (
                acc_s[h * t:(h + 1) * t, h * HD:(h + 1) * HD] / lcol[h * t:(h + 1) * t, :])


def fox_sample(page_table, zs, kn_pad, vn_pad, lfn_pad, cache_k, cache_v, cache_lf, *, layer, n, t, nh, qcol, pp):
    w = nh * HD
    n_pages = page_table.shape[1]
    nj = 1 + n_pages // pp
    later = jnp.asarray(np.triu(np.ones((pp * HD, pp * HD), np.float32), 1), BF16)

    def pg(i, tail):
        return lambda b, j, pt: (layer, pt[b, n_pages - jnp.maximum(j, 1) * pp + i]) + tail

    in_specs = [
        pl.BlockSpec((t, w), lambda b, j, pt: (b, qcol)),
        pl.BlockSpec((None, HD, w), lambda b, j, pt: (b, 0, 0)),
        pl.BlockSpec((None, HD, w), lambda b, j, pt: (b, 0, 0)),
        pl.BlockSpec((None, HD, nh), lambda b, j, pt: (b, 0, 0)),
        pl.BlockSpec(later.shape, lambda b, j, pt: (0, 0)),
    ]
    in_specs += [pl.BlockSpec((None, None, HD, nh, HD), pg(i, (0, 0, 0))) for i in range(pp)]
    in_specs += [pl.BlockSpec((None, None, HD, nh, HD), pg(i, (0, 0, 0))) for i in range(pp)]
    in_specs += [pl.BlockSpec((None, None, HD, nh), pg(i, (0, 0))) for i in range(pp)]
    return pl.pallas_call(
        functools.partial(_fox_sample_kernel, pp=pp, t=t, nh=nh, scale=HD ** -0.5),
        grid_spec=pltpu.PrefetchScalarGridSpec(
            num_scalar_prefetch=1,
            grid=(n, nj),
            in_specs=in_specs,
            out_specs=pl.BlockSpec((t, w), lambda b, j, pt: (b, 0)),
            scratch_shapes=[
                pltpu.VMEM((HD, w), BF16),
                pltpu.VMEM((1, HD), F32),
                pltpu.VMEM((1, HD), F32),
                pltpu.VMEM((1, HD), F32),
                pltpu.VMEM((1, HD), F32),
                pltpu.VMEM((HD, w), F32),
            ],
        ),
        out_shape=jax.ShapeDtypeStruct((n * t, w), F32),
        compiler_params=_params("parallel", "arbitrary"),
        name="fox_sample",
    )(page_table, zs, kn_pad, vn_pad, lfn_pad, later,
      *([cache_k] * pp), *([cache_v] * pp), *([cache_lf] * pp))


HG_CHUNK = 128
HG_LEVELS = 7


def _hgrn_sum_matrix():
    c = HG_CHUNK
    t = np.arange(c)[:, None]
    i = np.arange(c)[None, :]
    blocks = [i <= t, i > t]
    for lv in range(1, HG_LEVELS + 1):
        s = 1 << lv
        half = s // 2
        start = (t // s) * s
        upper = (t % s) >= half
        blocks.append(upper & (i >= start + half) & (i <= t))
        blocks.append((~upper) & (i > t) & (i <= start + half - 1))
    d = np.concatenate(blocks, axis=0).astype(np.float32)
    return np.concatenate([d, d, d], axis=1)


def _hgrn_kernel(q_ref, f_ref, i_ref, g_ref, s0_ref, lbl_ref, gn_ref, dm_ref,
                 y_ref, so_ref, st_s, pad_s, *, t_blk, nh, layer):
    c = HG_CHUNK
    cb = pl.program_id(1)

    @pl.when(cb == 0)
    def _():
        st_s[...] = s0_ref[...]

    lg = lbl_ref[...]
    e = jnp.exp(lg - jnp.max(lg, axis=0, keepdims=True))
    sm = e / jnp.sum(e, axis=0, keepdims=True)
    lb = jnp.sum(sm[1:layer + 1, :], axis=0, keepdims=True)
    log_lb = jnp.log(lb)
    log_1mlb = jnp.log1p(-lb)

    def load(ref, k):
        if t_blk == c:
            return ref[...]
        pad_s[k] = jnp.zeros(pad_s.shape[1:], F32)
        pad_s[k, 0:t_blk, :] = ref[...]
        return pad_s[k]

    q_all, f_all, v_all = load(q_ref, 0), load(f_ref, 1), load(i_ref, 2)

    row = lax.broadcasted_iota(jnp.int32, (c, HD), 0)
    rr = lax.broadcasted_iota(jnp.int32, (c, c), 0)
    cc = lax.broadcasted_iota(jnp.int32, (c, c), 1)
    eye_b = rr == cc
    eye = eye_b.astype(F32)
    live = row < t_blk
    dm = dm_ref[...]

    for h in range(nh):
        sl = slice(h * HD, (h + 1) * HD)
        q, cf, v = q_all[:, sl], f_all[:, sl], v_all[:, sl]
        x1 = log_lb[:, sl]
        x2 = log_1mlb[:, sl] + _log_sigmoid(cf)
        g = jnp.maximum(x1, x2) + jnp.log1p(jnp.exp(-jnp.abs(x1 - x2)))
        kk = (1.0 - lb[:, sl]) * jax.nn.sigmoid(-cf)
        if t_blk != c:
            g = jnp.where(live, g, 0.0)
            kk = jnp.where(live, kk, 0.0)
        hi = g.astype(BF16)
        r1 = g - hi.astype(F32)
        mid = r1.astype(BF16)
        lo = (r1 - mid.astype(F32)).astype(BF16)
        sums = jnp.dot(dm, jnp.concatenate([hi, mid, lo], axis=0), preferred_element_type=F32)
        b = sums[0:c]
        suffix = sums[c:2 * c]
        state = st_s[h]
        vb = v.astype(BF16)
        o = jnp.dot((q * jnp.exp(b)).astype(BF16), state.astype(BF16), preferred_element_type=F32)
        att = jnp.where(eye_b, _dot_nt(q.astype(BF16), kk.astype(BF16)), 0.0)
        for lv in range(1, HG_LEVELS + 1):
            s = 1 << lv
            upper = (row & (s - 1)) >= (s // 2)
            dq = sums[2 * lv * c:(2 * lv + 1) * c]
            dk = sums[(2 * lv + 1) * c:(2 * lv + 2) * c]
            qt = (q * jnp.where(upper, jnp.exp(dq), 0.0)).astype(BF16)
            kt = (kk * jnp.where(upper, 0.0, jnp.exp(dk))).astype(BF16)
            att = att + jnp.where((rr >> lv) == (cc >> lv), _dot_nt(qt, kt), 0.0)
        o = o + jnp.dot(att.astype(BF16), vb, preferred_element_type=F32)
        kd = kk * jnp.exp(suffix)
        ecol = _col_from_row(jnp.exp(b[c - 1:c, :]), eye)
        st_s[h] = ecol * state + jnp.dot(kd.T.astype(BF16), vb, preferred_element_type=F32)
        gate = g_ref[:, sl]
        oc = _rms(o[0:t_blk], gn_ref[...]) * (gate * jax.nn.sigmoid(gate))
        y_ref[:, sl] = oc

    so_ref[...] = st_s[...]


def hgrn_mixer(z, s0, lb_logits, g_norm, *, n, t, row0, layer):
    nh = s0.shape[1]
    w = nh * HD
    t_blk = min(t, HG_CHUNK)
    nc = t // t_blk
    rb0 = row0 // t_blk
    dm = jnp.asarray(_hgrn_sum_matrix(), BF16)

    def col(k):
        return pl.BlockSpec((t_blk, w), lambda i, b: (rb0 + i * nc + b, k))

    return pl.pallas_call(
        functools.partial(_hgrn_kernel, t_blk=t_blk, nh=nh, layer=layer),
        grid=(n, nc),
        in_specs=[
            col(0), col(1), col(2), col(3),
            pl.BlockSpec((None, nh, HD, HD), lambda i, b: (i, 0, 0, 0)),
            pl.BlockSpec(lb_logits.shape, lambda i, b: (0, 0)),
            pl.BlockSpec((1, HD), lambda i, b: (0, 0)),
            pl.BlockSpec(dm.shape, lambda i, b: (0, 0)),
        ],
        out_specs=[
            pl.BlockSpec((t_blk, w), lambda i, b: (i * nc + b, 0)),
            pl.BlockSpec((None, nh, HD, HD), lambda i, b: (i, 0, 0, 0)),
        ],
        out_shape=[
            jax.ShapeDtypeStruct((n * t, w), F32),
            jax.ShapeDtypeStruct((n, nh, HD, HD), F32),
        ],
        scratch_shapes=[
            pltpu.VMEM((nh, HD, HD), F32),
            pltpu.VMEM((3, HG_CHUNK, w), F32),
        ],
        compiler_params=_params("parallel", "arbitrary"),
        name="hgrn_mixer",
    )(z, z, z, z, s0, lb_logits, g_norm.reshape(1, HD), dm)


def _merge_groups(outs, lses):
    mx = functools.reduce(jnp.maximum, lses)
    ws = [jnp.exp(l - mx) for l in lses]
    den = functools.reduce(lambda a, b: a + b, ws)
    return functools.reduce(lambda a, b: a + b, [w * o for w, o in zip(ws, outs)]) / den


def _dsw_prompt_kernel(q0_ref, q1_ref, q2_ref, k_ref, v_ref, o_ref, m_s, l_s, acc_s, *, tq, nkv, nblk, scale):
    qrefs = (q0_ref, q1_ref, q2_ref)
    qi = pl.program_id(1)
    kb = pl.program_id(2)
    nkb = pl.num_programs(2)

    @pl.when(kb == 0)
    def _():
        m_s[...] = jnp.full_like(m_s, -jnp.inf)
        l_s[...] = jnp.zeros_like(l_s)
        acc_s[...] = jnp.zeros_like(acc_s)

    @pl.when(kb <= qi)
    def _():
        dist = (kb * tq + lax.broadcasted_iota(jnp.int32, (tq, tq), 0)
                - lax.broadcasted_iota(jnp.int32, (tq, tq), 1))
        for g, (win, dil) in enumerate(D_GROUPS):
            @pl.when(kb < nblk[g])
            def _():
                valid = (dist >= 0) & (dist <= win) & ((dist & (dil - 1)) == 0)
                for h in range(nkv):
                    sl = slice(h * HD, (h + 1) * HD)
                    idx = g * nkv + h
                    s = _dot_nt(qrefs[g][:, sl].astype(BF16), k_ref[:, sl].astype(BF16)) * scale
                    s = jnp.where(valid, s, -jnp.inf)
                    m_prev = m_s[idx]
                    m_new = jnp.maximum(m_prev, jnp.max(s, axis=-1, keepdims=True))
                    alpha = jnp.exp(m_prev - m_new)
                    p = jnp.exp(s - m_new)
                    l_s[idx] = alpha * l_s[idx] + jnp.sum(p, axis=-1, keepdims=True)
                    acc_s[idx] = alpha * acc_s[idx] + jnp.dot(
                        p.astype(BF16), v_ref[:, sl].astype(BF16), preferred_element_type=F32)
                    m_s[idx] = m_new

    @pl.when(kb == nkb - 1)
    def _():
        for h in range(nkv):
            idxs = [g * nkv + h for g in range(len(D_GROUPS))]
            outs = [acc_s[i] / l_s[i] for i in idxs]
            lses = [m_s[i] + jnp.log(l_s[i]) for i in idxs]
            o_ref[:, h * HD:(h + 1) * HD] = _merge_groups(outs, lses)


def dsw_prompt(z, *, n, s_len, tq, nkv, qcol):
    w = nkv * HD
    nq = s_len // tq
    ng = len(D_GROUPS)
    nblk = tuple(-(-win // tq) + 1 for win, _ in D_GROUPS)
    nkb = min(max(nblk), nq)
    qspec = lambda g: pl.BlockSpec((tq, w), lambda i, a, b: (i * nq + a, qcol + g))
    kv = lambda k: pl.BlockSpec((tq, w), lambda i, a, b: (i * nq + jnp.maximum(a - b, 0), qcol + ng + k))
    return pl.pallas_call(
        functools.partial(_dsw_prompt_kernel, tq=tq, nkv=nkv, nblk=nblk, scale=HD ** -0.5),
        grid=(n, nq, nkb),
        in_specs=[qspec(0), qspec(1), qspec(2), kv(0), kv(1)],
        out_specs=pl.BlockSpec((tq, w), lambda i, a, b: (i * nq + a, 0)),
        out_shape=jax.ShapeDtypeStruct((n * s_len, w), F32),
        scratch_shapes=[
            pltpu.VMEM((ng * nkv, tq, 1), F32),
            pltpu.VMEM((ng * nkv, tq, 1), F32),
            pltpu.VMEM((ng * nkv, tq, HD), F32),
        ],
        compiler_params=_params("parallel", "parallel", "arbitrary"),
        name="dsw_prompt",
    )(z, z, z, z, z)


def _dsw_sample_kernel(q0_ref, q1_ref, q2_ref, kn_ref, vn_ref, kc_ref, vc_ref, o_ref, kn_s, vn_s,
                       *, t, nkv, wbuf, scale):
    ng = len(D_GROUPS)
    kn_s[...] = jnp.zeros_like(kn_s)
    vn_s[...] = jnp.zeros_like(vn_s)
    kn_s[0:t, :] = kn_ref[...]
    vn_s[0:t, :] = vn_ref[...]

    def group_consts(shape):
        r = lax.broadcasted_iota(jnp.int32, shape, 0)
        grp = r // t
        win = jnp.zeros(shape, jnp.int32)
        dmask = jnp.zeros(shape, jnp.int32)
        for g, (w_, d_) in enumerate(D_GROUPS):
            win = jnp.where(grp == g, w_, win)
            dmask = jnp.where(grp == g, d_ - 1, dmask)
        return r % t, win, dmask

    tok_c, win_c, dm_c = group_consts((ng * t, wbuf))
    dist_c = wbuf + tok_c - lax.broadcasted_iota(jnp.int32, (ng * t, wbuf), 1)
    valid_c = (dist_c >= 0) & (dist_c <= win_c) & ((dist_c & dm_c) == 0)
    tok_n, win_n, dm_n = group_consts((ng * t, HD))
    dist_n = tok_n - lax.broadcasted_iota(jnp.int32, (ng * t, HD), 1)
    valid_n = (dist_n >= 0) & (dist_n <= win_n) & ((dist_n & dm_n) == 0)

    for h in range(nkv):
        sl = slice(h * HD, (h + 1) * HD)
        q = jnp.concatenate([q0_ref[:, sl], q1_ref[:, sl], q2_ref[:, sl]], axis=0).astype(BF16)
        s_c = jnp.where(valid_c, _dot_nt(q, kc_ref[:, h, :].astype(BF16)) * scale, -jnp.inf)
        s_n = jnp.where(valid_n, _dot_nt(q, kn_s[:, sl].astype(BF16)) * scale, -jnp.inf)
        m = jnp.maximum(jnp.max(s_c, axis=-1, keepdims=True), jnp.max(s_n, axis=-1, keepdims=True))
        p_c = jnp.exp(s_c - m)
        p_n = jnp.exp(s_n - m)
        l = jnp.sum(p_c, axis=-1, keepdims=True) + jnp.sum(p_n, axis=-1, keepdims=True)
        o = jnp.dot(p_c.astype(BF16), vc_ref[:, h, :].astype(BF16), preferred_element_type=F32)
        o = (o + jnp.dot(p_n.astype(BF16), vn_s[:, sl].astype(BF16), preferred_element_type=F32)) / l
        lse = m + jnp.log(l)
        o_ref[:, sl] = _merge_groups([o[g * t:(g + 1) * t] for g in range(ng)],
                                     [lse[g * t:(g + 1) * t] for g in range(ng)])


def dsw_sample(z, cache_k, cache_v, *, layer, n, t, nkv, qcol, row0):
    w = nkv * HD
    wbuf = cache_k.shape[2]
    ng = len(D_GROUPS)
    rb0 = row0 // t
    col = lambda k: pl.BlockSpec((t, w), lambda i: (rb0 + i, qcol + k))
    cache = pl.BlockSpec((None, None, wbuf, nkv, HD), lambda i: (layer, i, 0, 0, 0))
    return pl.pallas_call(
        functools.partial(_dsw_sample_kernel, t=t, nkv=nkv, wbuf=wbuf, scale=HD ** -0.5),
        grid=(n,),
        in_specs=[col(0), col(1), col(2), col(ng), col(ng + 1), cache, cache],
        out_specs=pl.BlockSpec((t, w), lambda i: (i, 0)),
        out_shape=jax.ShapeDtypeStruct((n * t, w), F32),
        scratch_shapes=[pltpu.VMEM((HD, w), F32), pltpu.VMEM((HD, w), F32)],
        compiler_params=_params("parallel"),
        name="dsw_sample",
    )(z, z, z, z, z, cache_k, cache_v)


def kernel(x_prompt, x_sample, cache_fox_k, cache_fox_v, cache_fox_logf, page_table, state_lru_conv, state_lru_h, state_hgrn, cache_dsw_k, cache_dsw_v, norm_gains, ffn_w_gate, ffn_w_up, ffn_w_down, even_w_in, even_w_out, lru_conv_w, lru_conv_b, lru_w_r, lru_b_r, lru_w_i, lru_b_i, lru_lambda, fox_b_f, odd_w_in, odd_w_out, hgrn_lb_logits, hgrn_norm, final_norm):
    n_p, s_len, d = x_prompt.shape
    n_s, t_s, _ = x_sample.shape
    depth = norm_gains.shape[0]
    a_w = lru_lambda.shape[-1]
    b_heads = fox_b_f.shape[-1]
    b_w = b_heads * HD
    c_heads = state_hgrn.shape[2]
    c_w = c_heads * HD
    d_heads = cache_dsw_k.shape[3]
    d_w = d_heads * HD
    n_pool = cache_fox_k.shape[1]
    wbuf = cache_dsw_k.shape[2]
    keep = min(max(w for w, _ in D_GROUPS), s_len)

    xp = x_prompt.reshape(n_p * s_len, d)
    xs = x_sample.reshape(n_s * t_s, d)
    tm_p, tm_s = 1024, n_s * t_s
    names = ('fox_k', 'fox_v', 'fox_logf', 'lru_conv', 'lru_h', 'hgrn', 'dsw_k', 'dsw_v')
    st_p = {nm: [] for nm in names}
    st_s = {nm: [] for nm in names}

    def ffn_both(xp, xs, l, k, final):
        g = norm_gains[l, 2 * k]
        w3 = (ffn_w_gate, ffn_w_up, ffn_w_down)
        return (ffn(xp, g, *w3, final_norm, l=l, k=k, final=final, tm=tm_p, tf=256),
                ffn(xs, g, *w3, final_norm, l=l, k=k, final=final, tm=tm_s, tf=512))

    for l in range(depth):
        j = l // 2
        xp, xs = ffn_both(xp, xs, l, 0, False)
        g_mix = norm_gains[l, 1]
        if l % 2 == 0:
            n_main = 2 * a_w + 3 * b_w
            w_in = even_w_in[j]
            w_f = jnp.pad(w_in[:, n_main:], ((0, 0), (0, HD - b_heads)))
            bfp = jnp.pad(fox_b_f[j], (0, HD - b_heads)).reshape(1, HD)
            wri = jnp.concatenate([lru_w_r[j], lru_w_i[j]], axis=-1).astype(BF16)
            w_out = even_w_out[j]
            lru_args = (lru_conv_w[j], lru_conv_b[j], wri, lru_b_r[j], lru_b_i[j], lru_lambda[j])
            qcol = 2 * a_w // b_w

            zp = norm_matmul(xp, g_mix, w_in, tm=tm_p, tn=512)
            zfp = norm_matmul(xp, g_mix, w_f, tm=tm_p, tn=HD)
            zs = norm_matmul(xs, g_mix, w_in, tm=tm_s, tn=512)
            zfs = norm_matmul(xs, g_mix, w_f, tm=tm_s, tn=HD)

            ya_p, cv_p, h_p = lru_mixer(zp, jnp.zeros((n_p, CONV_W - 1, a_w), F32), jnp.zeros((n_p, a_w), F32),
                                        *lru_args, n=n_p, t=s_len, tt=256, starts=True)
            lf_p, cum_p, cumt_p = fox_prep(zfp, bfp, n=n_p, s_len=s_len, nh=b_heads)
            ob_p = fox_prompt(zp, cum_p, cumt_p, n=n_p, s_len=s_len, tq=512, qcol=qcol)

            ya_s, cv_s, h_s = lru_mixer(zs, state_lru_conv[j], state_lru_h[j],
                                        *lru_args, n=n_s, t=t_s, tt=t_s, starts=False)
            lf_s, _, _ = fox_prep(zfs, bfp, n=1, s_len=n_s * t_s, nh=b_heads)
            k_s = zs[:, 2 * a_w + b_w:2 * a_w + 2 * b_w]
            v_s = zs[:, 2 * a_w + 2 * b_w:2 * a_w + 3 * b_w]
            pad_rows = lambda a: jnp.pad(a.reshape(n_s, t_s, -1), ((0, 0), (0, HD - t_s), (0, 0)))
            ob_s = fox_sample(page_table, zs, pad_rows(k_s), pad_rows(v_s), pad_rows(lf_s),
                              cache_fox_k, cache_fox_v, cache_fox_logf,
                              layer=j, n=n_s, t=t_s, nh=b_heads, qcol=qcol, pp=8)

            xp = out_proj(xp, ya_p, ob_p, w_out, tm=tm_p, tn=512)
            xs = out_proj(xs, ya_s, ob_s, w_out, tm=tm_s, tn=512)

            k_p = zp[:, 2 * a_w + b_w:2 * a_w + 2 * b_w]
            v_p = zp[:, 2 * a_w + 2 * b_w:2 * a_w + 3 * b_w]
            st_p['fox_k'].append(k_p.reshape(n_p, s_len, b_heads, HD))
            st_p['fox_v'].append(v_p.reshape(n_p, s_len, b_heads, HD))
            st_p['fox_logf'].append(lf_p.reshape(n_p, s_len, b_heads))
            st_p['lru_conv'].append(cv_p)
            st_p['lru_h'].append(h_p.reshape(n_p, a_w))
            st_s['fox_k'].append(k_s.reshape(n_s, t_s, b_heads, HD))
            st_s['fox_v'].append(v_s.reshape(n_s, t_s, b_heads, HD))
            st_s['fox_logf'].append(lf_s.reshape(n_s, t_s, b_heads))
            st_s['lru_conv'].append(cv_s)
            st_s['lru_h'].append(h_s.reshape(n_s, a_w))
        else:
            w_in = odd_w_in[j]
            w_out = odd_w_out[j]
            qcol = 4 * c_w // d_w
            zp = norm_matmul(xp, g_mix, w_in, tm=tm_p, tn=512)
            zs = norm_matmul(xs, g_mix, w_in, tm=tm_s, tn=512)

            oc_p, s_p = hgrn_mixer(zp, jnp.zeros((n_p, c_heads, HD, HD), F32), hgrn_lb_logits, hgrn_norm[j],
                                   n=n_p, t=s_len, row0=0, layer=l)
            od_p = dsw_prompt(zp, n=n_p, s_len=s_len, tq=512, nkv=d_heads, qcol=qcol)
            oc_s, s_s = hgrn_mixer(zs, state_hgrn[j], hgrn_lb_logits, hgrn_norm[j],
                                   n=n_s, t=t_s, row0=0, layer=l)
            od_s = dsw_sample(zs, cache_dsw_k, cache_dsw_v,
                              layer=j, n=n_s, t=t_s, nkv=d_heads, qcol=qcol, row0=0)

            xp = out_proj(xp, oc_p, od_p, w_out, tm=tm_p, tn=512)
            xs = out_proj(xs, oc_s, od_s, w_out, tm=tm_s, tn=512)

            k0 = 4 * c_w + len(D_GROUPS) * d_w
            zp3 = zp.reshape(n_p, s_len, -1)
            st_p['hgrn'].append(s_p)
            st_p['dsw_k'].append(zp3[:, s_len - keep:, k0:k0 + d_w].reshape(n_p, keep, d_heads, HD))
            st_p['dsw_v'].append(zp3[:, s_len - keep:, k0 + d_w:k0 + 2 * d_w].reshape(n_p, keep, d_heads, HD))
            st_s['hgrn'].append(s_s)
            st_s['dsw_k'].append(zs[:, k0:k0 + d_w].reshape(n_s, t_s, d_heads, HD))
            st_s['dsw_v'].append(zs[:, k0 + d_w:k0 + 2 * d_w].reshape(n_s, t_s, d_heads, HD))
        xp, xs = ffn_both(xp, xs, l, 1, l == depth - 1)

    y_prompt = xp.reshape(n_p, s_len, d)
    y_sample = xs.reshape(n_s, t_s, d)
    return (y_prompt, y_sample,
            *(jnp.stack(st_p[nm]) for nm in names),
            *(jnp.stack(st_s[nm]) for nm in names))
```

```python
import functools

import numpy as np
import jax
import jax.numpy as jnp
from jax import lax
from jax.experimental import pallas as pl
from jax.experimental.pallas import tpu as pltpu

F32 = jnp.float32
BF16 = jnp.bfloat16
EPS = 1e-6
HD = 128
LRU_C = 8.0
CONV_W = 4
D_GROUPS = ((128, 1), (512, 4), (2048, 16))
VMEM_LIMIT_BYTES = 56 * 1024 * 1024
HIGHEST = lax.Precision.HIGHEST
NT_DIMS = (((1,), (1,)), ((), ()))


def _params(*sem):
    return pltpu.CompilerParams(dimension_semantics=sem, vmem_limit_bytes=VMEM_LIMIT_BYTES)


def _rms(x, g):
    ms = jnp.mean(x * x, axis=-1, keepdims=True)
    return x * lax.rsqrt(ms + EPS) * g


def _softplus(z):
    return jnp.maximum(z, 0.0) + jnp.log1p(jnp.exp(-jnp.abs(z)))


def _log_sigmoid(z):
    return -_softplus(-z)


def _dot_nt(a, b):
    return lax.dot_general(a, b, NT_DIMS, preferred_element_type=F32)


def _col_from_row(row, eye):
    return jnp.sum(eye * row, axis=1, keepdims=True)


def _ffn_kernel(x_ref, g_ref, wg_ref, wu_ref, wd_ref, gf_ref, o_ref, xn_ref, *, nf, final):
    f = pl.program_id(1)

    @pl.when(f == 0)
    def _():
        xn_ref[...] = _rms(x_ref[...], g_ref[...]).astype(BF16)
        o_ref[...] = jnp.zeros_like(o_ref)

    xn = xn_ref[...]
    a = jnp.dot(xn, wg_ref[...].astype(BF16), preferred_element_type=F32)
    b = jnp.dot(xn, wu_ref[...].astype(BF16), preferred_element_type=F32)
    h = (a * jax.nn.sigmoid(a) * b).astype(BF16)
    o_ref[...] += jnp.dot(h, wd_ref[...].astype(BF16), preferred_element_type=F32)

    @pl.when(f == nf - 1)
    def _():
        y = x_ref[...] + 0.5 * o_ref[...]
        if final:
            y = _rms(y, gf_ref[...])
        o_ref[...] = y


def ffn(x, g, wg, wu, wd, gf, *, l, k, final, tm, tf):
    m, d = x.shape
    nf = wg.shape[-1] // tf
    return pl.pallas_call(
        functools.partial(_ffn_kernel, nf=nf, final=final),
        grid=(m // tm, nf),
        in_specs=[
            pl.BlockSpec((tm, d), lambda i, f: (i, 0), pipeline_mode=pl.Buffered(1)),
            pl.BlockSpec((1, d), lambda i, f: (0, 0)),
            pl.BlockSpec((None, None, d, tf), lambda i, f: (l, k, 0, f)),
            pl.BlockSpec((None, None, d, tf), lambda i, f: (l, k, 0, f)),
            pl.BlockSpec((None, None, tf, d), lambda i, f: (l, k, f, 0)),
            pl.BlockSpec((1, d), lambda i, f: (0, 0)),
        ],
        out_specs=pl.BlockSpec((tm, d), lambda i, f: (i, 0)),
        out_shape=jax.ShapeDtypeStruct((m, d), F32),
        scratch_shapes=[pltpu.VMEM((tm, d), BF16)],
        compiler_params=_params("parallel", "arbitrary"),
        name="ffn",
    )(x, g.reshape(1, d), wg, wu, wd, gf.reshape(1, d))


def _norm_matmul_kernel(x_ref, g_ref, w_ref, o_ref, xn_ref):
    @pl.when(pl.program_id(1) == 0)
    def _():
        xn_ref[...] = _rms(x_ref[...], g_ref[...]).astype(BF16)

    o_ref[...] = jnp.dot(xn_ref[...], w_ref[...].astype(BF16), preferred_element_type=F32)


def norm_matmul(x, g, w, *, tm, tn):
    m, d = x.shape
    n = w.shape[1] // tn * tn
    return pl.pallas_call(
        _norm_matmul_kernel,
        grid=(m // tm, n // tn),
        in_specs=[
            pl.BlockSpec((tm, d), lambda i, j: (i, 0), pipeline_mode=pl.Buffered(1)),
            pl.BlockSpec((1, d), lambda i, j: (0, 0)),
            pl.BlockSpec((d, tn), lambda i, j: (0, j)),
        ],
        out_specs=pl.BlockSpec((tm, tn), lambda i, j: (i, j)),
        out_shape=jax.ShapeDtypeStruct((m, n), F32),
        scratch_shapes=[pltpu.VMEM((tm, d), BF16)],
        compiler_params=_params("parallel", "arbitrary"),
        name="norm_matmul",
    )(x, g.reshape(1, d), w)


def _out_proj_kernel(r_ref, ya_ref, yb_ref, wa_ref, wb_ref, o_ref):
    acc = jnp.dot(ya_ref[...].astype(BF16), wa_ref[...].astype(BF16), preferred_element_type=F32)
    acc = acc + jnp.dot(yb_ref[...].astype(BF16), wb_ref[...].astype(BF16), preferred_element_type=F32)
    o_ref[...] = r_ref[...] + acc


def out_proj(res, ya, yb, w, *, tm, tn):
    m, d = res.shape
    ka, kb = ya.shape[1], yb.shape[1]
    assert ka % kb == 0
    return pl.pallas_call(
        _out_proj_kernel,
        grid=(m // tm, d // tn),
        in_specs=[
            pl.BlockSpec((tm, tn), lambda i, j: (i, j)),
            pl.BlockSpec((tm, ka), lambda i, j: (i, 0)),
            pl.BlockSpec((tm, kb), lambda i, j: (i, 0)),
            pl.BlockSpec((ka, tn), lambda i, j: (0, j)),
            pl.BlockSpec((kb, tn), lambda i, j: (ka // kb, j)),
        ],
        out_specs=pl.BlockSpec((tm, tn), lambda i, j: (i, j)),
        out_shape=jax.ShapeDtypeStruct((m, d), F32),
        compiler_params=_params("parallel", "arbitrary"),
        name="out_proj",
    )(res, ya, yb, w, w)


def _lru_kernel(ax_ref, ag_ref, cb0_ref, h0_ref, cw_ref, cb_ref, wri_ref, br_ref, bi_ref, lam_ref,
                ya_ref, cst_ref, hl_ref, xbuf, hcar, a_s, b_s, *, tt, nh, starts):
    tb = pl.program_id(1)

    @pl.when(tb == 0)
    def _():
        xbuf[5:8, :] = cb0_ref[...]
        hcar[...] = h0_ref[...]

    x = ax_ref[...]
    xbuf[8:8 + tt, :] = x
    cw = cw_ref[...]
    y = cb_ref[...] + xbuf[5:5 + tt, :] * cw[0:1, :]
    y = y + xbuf[6:6 + tt, :] * cw[1:2, :]
    y = y + xbuf[7:7 + tt, :] * cw[2:3, :]
    y = y + x * cw[3:4, :]
    tail = xbuf[5 + tt:8 + tt, :]
    xbuf[5:8, :] = tail
    cst_ref[...] = tail

    sp = _softplus(-lam_ref[...])
    for h in range(nh):
        sl = slice(h * HD, (h + 1) * HD)
        yh = y[:, sl]
        gts = jnp.dot(yh.astype(BF16), wri_ref[h], preferred_element_type=F32)
        gr = jax.nn.sigmoid(gts[:, :HD] + br_ref[:, sl])
        gi = jax.nn.sigmoid(gts[:, HD:] + bi_ref[:, sl])
        log_a = -LRU_C * gr * sp[:, sl]
        a = jnp.exp(log_a)
        mult = jnp.sqrt(-jnp.tanh(log_a) * (a * a + 1.0))
        if starts:
            row = lax.broadcasted_iota(jnp.int32, mult.shape, 0)
            mult = jnp.where((row == 0) & (tb == 0), 1.0, mult)
        a_s[:, sl] = a
        b_s[:, sl] = mult * gi * yh

    def group(gidx, hprev):
        r = pl.multiple_of(gidx * 8, 8)
        a8 = a_s[pl.ds(r, 8), :]
        b8 = b_s[pl.ds(r, 8), :]
        row = lax.broadcasted_iota(jnp.int32, a8.shape, 0)
        for s in (1, 2, 4):
            ok = row >= s
            b8 = jnp.where(ok, a8 * pltpu.roll(b8, s, 0) + b8, b8)
            a8 = jnp.where(ok, a8 * pltpu.roll(a8, s, 0), a8)
        h8 = a8 * hprev + b8
        b_s[pl.ds(r, 8), :] = h8
        return h8[7:8, :]

    hlast = lax.fori_loop(0, tt // 8, group, hcar[...])
    hcar[...] = hlast
    hl_ref[...] = hlast
    ya_ref[...] = b_s[...] * jax.nn.gelu(ag_ref[...])


def lru_mixer(z, conv0, h0, cw, cb, wri, br, bi, lam, *, n, t, tt, starts):
    w = lam.shape[-1]
    nh = w // HD
    nb = t // tt
    row = lambda i, b: (i * nb + b, 0)
    vec = pl.BlockSpec((1, w), lambda i, b: (0, 0))
    return pl.pallas_call(
        functools.partial(_lru_kernel, tt=tt, nh=nh, starts=starts),
        grid=(n, nb),
        in_specs=[
            pl.BlockSpec((tt, w), row),
            pl.BlockSpec((tt, w), lambda i, b: (i * nb + b, 1)),
            pl.BlockSpec((None, CONV_W - 1, w), lambda i, b: (i, 0, 0)),
            pl.BlockSpec((None, 1, w), lambda i, b: (i, 0, 0)),
            pl.BlockSpec((CONV_W, w), lambda i, b: (0, 0)),
            vec,
            pl.BlockSpec((nh, HD, 2 * HD), lambda i, b: (0, 0, 0)),
            vec, vec, vec,
        ],
        out_specs=[
            pl.BlockSpec((tt, w), row),
            pl.BlockSpec((None, CONV_W - 1, w), lambda i, b: (i, 0, 0)),
            pl.BlockSpec((None, 1, w), lambda i, b: (i, 0, 0)),
        ],
        out_shape=[
            jax.ShapeDtypeStruct((n * t, w), F32),
            jax.ShapeDtypeStruct((n, CONV_W - 1, w), F32),
            jax.ShapeDtypeStruct((n, 1, w), F32),
        ],
        scratch_shapes=[
            pltpu.VMEM((tt + 8, w), F32),
            pltpu.VMEM((1, w), F32),
            pltpu.VMEM((tt, w), F32),
            pltpu.VMEM((tt, w), F32),
        ],
        compiler_params=_params("parallel", "arbitrary"),
        name="lru_mixer",
    )(z, z, conv0, h0.reshape(n, 1, w), cw, cb.reshape(1, w), wri,
      br.reshape(1, w), bi.reshape(1, w), lam.reshape(1, w))


def _fox_prep_kernel(zf_ref, bf_ref, lf_ref, cum_ref, cumt_ref, cum_s, cumt_s, *, s_len, nh):
    lf = _log_sigmoid(zf_ref[...] + bf_ref[...])
    lf_ref[...] = lf[:, :nh]
    cum_s[...] = lf
    r = lax.broadcasted_iota(jnp.int32, (HD, HD), 0)
    c = lax.broadcasted_iota(jnp.int32, (HD, HD), 1)
    tri = (r >= c).astype(F32)

    def chunk(i, carry):
        st = pl.multiple_of(i * HD, HD)
        cs = jnp.dot(tri, cum_s[pl.ds(st, HD), :], preferred_element_type=F32, precision=HIGHEST) + carry
        cum_s[pl.ds(st, HD), :] = cs
        cumt_s[:, pl.ds(st, HD)] = cs.T
        return cs[HD - 1:HD, :]

    lax.fori_loop(0, s_len // HD, chunk, jnp.zeros((1, HD), F32))
    cum_ref[...] = cum_s[:, :nh]
    cumt_ref[...] = cumt_s[:nh, :]


def fox_prep(zf, bfp, *, n, s_len, nh):
    return pl.pallas_call(
        functools.partial(_fox_prep_kernel, s_len=s_len, nh=nh),
        grid=(n,),
        in_specs=[
            pl.BlockSpec((s_len, HD), lambda i: (i, 0)),
            pl.BlockSpec((1, HD), lambda i: (0, 0)),
        ],
        out_specs=[
            pl.BlockSpec((s_len, nh), lambda i: (i, 0)),
            pl.BlockSpec((s_len, nh), lambda i: (i, 0)),
            pl.BlockSpec((None, nh, s_len), lambda i: (i, 0, 0)),
        ],
        out_shape=[
            jax.ShapeDtypeStruct((n * s_len, nh), F32),
            jax.ShapeDtypeStruct((n * s_len, nh), F32),
            jax.ShapeDtypeStruct((n, nh, s_len), F32),
        ],
        scratch_shapes=[pltpu.VMEM((s_len, HD), F32), pltpu.VMEM((HD, s_len), F32)],
        compiler_params=_params("parallel"),
        name="fox_prep",
    )(zf, bfp)


def _fox_prompt_kernel(q_ref, k_ref, v_ref, cq_ref, ck_ref, o_ref, m_s, l_s, acc_s, *, tq, nh, scale):
    qi = pl.program_id(1)
    kj = pl.program_id(2)

    @pl.when(kj == 0)
    def _():
        m_s[...] = jnp.full_like(m_s, -jnp.inf)
        l_s[...] = jnp.zeros_like(l_s)
        acc_s[...] = jnp.zeros_like(acc_s)

    def block(causal):
        for h in range(nh):
            sl = slice(h * HD, (h + 1) * HD)
            s = _dot_nt(q_ref[:, sl].astype(BF16), k_ref[:, sl].astype(BF16)) * scale
            s = s + (cq_ref[:, h:h + 1] - ck_ref[h:h + 1, :])
            if causal is not None:
                s = jnp.where(causal, s, -jnp.inf)
            m_prev = m_s[h]
            m_new = jnp.maximum(m_prev, jnp.max(s, axis=-1, keepdims=True))
            alpha = jnp.exp(m_prev - m_new)
            p = jnp.exp(s - m_new)
            l_s[h] = alpha * l_s[h] + jnp.sum(p, axis=-1, keepdims=True)
            acc_s[:, sl] = alpha * acc_s[:, sl] + jnp.dot(
                p.astype(BF16), v_ref[:, sl].astype(BF16), preferred_element_type=F32)
            m_s[h] = m_new

    @pl.when(kj < qi)
    def _():
        block(None)

    @pl.when(kj == qi)
    def _():
        row = lax.broadcasted_iota(jnp.int32, (tq, tq), 0)
        col = lax.broadcasted_iota(jnp.int32, (tq, tq), 1)
        block(col <= row)
        for h in range(nh):
            sl = slice(h * HD, (h + 1) * HD)
            o_ref[:, sl] = acc_s[:, sl] / l_s[h]


def fox_prompt(z, cum, cumt, *, n, s_len, tq, qcol):
    nh = cum.shape[1]
    w = nh * HD
    nq = s_len // tq
    return pl.pallas_call(
        functools.partial(_fox_prompt_kernel, tq=tq, nh=nh, scale=HD ** -0.5),
        grid=(n, nq, nq),
        in_specs=[
            pl.BlockSpec((tq, w), lambda i, a, b: (i * nq + a, qcol)),
            pl.BlockSpec((tq, w), lambda i, a, b: (i * nq + jnp.minimum(a, b), qcol + 1)),
            pl.BlockSpec((tq, w), lambda i, a, b: (i * nq + jnp.minimum(a, b), qcol + 2)),
            pl.BlockSpec((tq, nh), lambda i, a, b: (i * nq + a, 0)),
            pl.BlockSpec((None, nh, tq), lambda i, a, b: (i, 0, jnp.minimum(a, b))),
        ],
        out_specs=pl.BlockSpec((tq, w), lambda i, a, b: (i * nq + a, 0)),
        out_shape=jax.ShapeDtypeStruct((n * s_len, w), F32),
        scratch_shapes=[
            pltpu.VMEM((nh, tq, 1), F32),
            pltpu.VMEM((nh, tq, 1), F32),
            pltpu.VMEM((tq, w), F32),
        ],
        compiler_params=_params("parallel", "parallel", "arbitrary"),
        name="fox_prompt",
    )(z, z, z, cum, cumt)


def _split3(x):
    hi = x.astype(BF16)
    r1 = x - hi.astype(F32)
    mid = r1.astype(BF16)
    return hi, mid, (r1 - mid.astype(F32)).astype(BF16)


def _fox_sample_kernel(pt_ref, q_ref, kn_ref, vn_ref, lfn_ref, later_ref, *refs, pp, t, nh, scale):
    kc, vc, lc = refs[0:pp], refs[pp:2 * pp], refs[2 * pp:3 * pp]
    o_ref = refs[3 * pp]
    qb_s, crow_s, carry_s, m_s, l_s, acc_s = refs[3 * pp + 1:]
    j = pl.program_id(1)
    nj = pl.num_programs(1)
    ncol = nh * t

    r128 = lax.broadcasted_iota(jnp.int32, (HD, HD), 0)
    c128 = lax.broadcasted_iota(jnp.int32, (HD, HD), 1)
    eye = (r128 == c128).astype(F32)
    hrow = lax.broadcasted_iota(jnp.int32, (nh, HD), 0)
    hcol = lax.broadcasted_iota(jnp.int32, (nh, HD), 1)
    expand = ((hcol // t == hrow) & (hcol < ncol)).astype(BF16)

    def to_columns(lf):
        return [jnp.dot(part, expand, preferred_element_type=F32).astype(BF16) for part in _split3(lf)]

    def block(k2d, v2d, lf, valid):
        rows = k2d.shape[0]
        parts = to_columns(lf)
        later_in_block = later_ref[0:rows, 0:rows]
        later = carry_s[...] + crow_s[...]
        total = jnp.zeros((1, HD), F32)
        for part in parts:
            later = later + jnp.dot(later_in_block, part, preferred_element_type=F32)
            total = total + jnp.sum(part.astype(F32), axis=0, keepdims=True)
        carry_s[...] = carry_s[...] + total
        s = _dot_nt(k2d, qb_s[...]) * scale + later
        if valid is not None:
            s = jnp.where(valid, s, -jnp.inf)
        m_prev = m_s[...]
        m_new = jnp.maximum(m_prev, jnp.max(s, axis=0, keepdims=True))
        alpha = jnp.exp(m_prev - m_new)
        p = jnp.exp(s - m_new)
        l_s[...] = alpha * l_s[...] + jnp.sum(p, axis=0, keepdims=True)
        m_s[...] = m_new
        acc_s[...] = _col_from_row(alpha, eye) * acc_s[...] + jnp.dot(
            p.T.astype(BF16), v2d, preferred_element_type=F32)

    def heads_to_lanes(ref):
        return jnp.concatenate([ref[pl.ds(h, HD, stride=nh), :] for h in range(nh)], axis=1).astype(BF16)

    @pl.when(j == 0)
    def _():
        q = q_ref[...]
        w = q.shape[1]
        qrep = jnp.concatenate([q] * nh + [jnp.zeros((HD - ncol, w), F32)], axis=0)
        rr = lax.broadcasted_iota(jnp.int32, (HD, w), 0)
        cc = lax.broadcasted_iota(jnp.int32, (HD, w), 1)
        qb_s[...] = jnp.where(rr // t == cc // HD, qrep, 0.0).astype(BF16)
        m_s[...] = jnp.full_like(m_s, -jnp.inf)
        l_s[...] = jnp.zeros_like(l_s)
        acc_s[...] = jnp.zeros_like(acc_s)
        carry_s[...] = jnp.zeros_like(carry_s)
        lfn = lfn_ref[...]
        lfn_cols = functools.reduce(lambda a, b: a + b, [p.astype(F32) for p in to_columns(lfn)])
        after = r128 > (c128 % t)
        crow_s[...] = -jnp.sum(jnp.where(after, lfn_cols, 0.0), axis=0, keepdims=True)
        block(kn_ref[...].astype(BF16), vn_ref[...].astype(BF16), lfn, r128 <= (c128 % t))

    @pl.when(j > 0)
    def _():
        block(jnp.concatenate([heads_to_lanes(r) for r in kc], axis=0),
              jnp.concatenate([heads_to_lanes(r) for r in vc], axis=0),
              jnp.concatenate([r[...] for r in lc], axis=0), None)

    @pl.when(j == nj - 1)
    def _():
        lcol = _col_from_row(l_s[...], eye)
        for h in range(nh):
            o_ref[:, h * HD:(h + 1) * HD] = (
                acc_s[h * t:(h + 1) * t, h * HD:(h + 1) * HD] / lcol[h * t:(h + 1) * t, :])


def fox_sample(page_table, zs, kn_pad, vn_pad, lfn_pad, cache_k, cache_v, cache_lf, *, layer, n, t, nh, qcol, pp):
    w = nh * HD
    n_layers, n_pool = cache_k.shape[:2]
    cache_k = cache_k.reshape(n_layers, n_pool, HD * nh, HD)
    cache_v = cache_v.reshape(n_layers, n_pool, HD * nh, HD)
    n_pages = page_table.shape[1]
    nj = 1 + n_pages // pp
    later = jnp.asarray(np.triu(np.ones((pp * HD, pp * HD), np.float32), 1), BF16)

    def pg(i, tail):
        return lambda b, j, pt: (layer, pt[b, n_pages - jnp.maximum(j, 1) * pp + i]) + tail

    in_specs = [
        pl.BlockSpec((t, w), lambda b, j, pt: (b, qcol)),
        pl.BlockSpec((None, HD, w), lambda b, j, pt: (b, 0, 0)),
        pl.BlockSpec((None, HD, w), lambda b, j, pt: (b, 0, 0)),
        pl.BlockSpec((None, HD, nh), lambda b, j, pt: (b, 0, 0)),
        pl.BlockSpec(later.shape, lambda b, j, pt: (0, 0)),
    ]
    in_specs += [pl.BlockSpec((None, None, HD * nh, HD), pg(i, (0, 0))) for i in range(pp)]
    in_specs += [pl.BlockSpec((None, None, HD * nh, HD), pg(i, (0, 0))) for i in range(pp)]
    in_specs += [pl.BlockSpec((None, None, HD, nh), pg(i, (0, 0))) for i in range(pp)]
    return pl.pallas_call(
        functools.partial(_fox_sample_kernel, pp=pp, t=t, nh=nh, scale=HD ** -0.5),
        grid_spec=pltpu.PrefetchScalarGridSpec(
            num_scalar_prefetch=1,
            grid=(n, nj),
            in_specs=in_specs,
            out_specs=pl.BlockSpec((t, w), lambda b, j, pt: (b, 0)),
            scratch_shapes=[
                pltpu.VMEM((HD, w), BF16),
                pltpu.VMEM((1, HD), F32),
                pltpu.VMEM((1, HD), F32),
                pltpu.VMEM((1, HD), F32),
                pltpu.VMEM((1, HD), F32),
                pltpu.VMEM((HD, w), F32),
            ],
        ),
        out_shape=jax.ShapeDtypeStruct((n * t, w), F32),
        compiler_params=_params("parallel", "arbitrary"),
        name="fox_sample",
    )(page_table, zs, kn_pad, vn_pad, lfn_pad, later,
      *([cache_k] * pp), *([cache_v] * pp), *([cache_lf] * pp))


HG_CHUNK = 128
HG_MATMUL_LEVELS = 3


def _hgrn_sum_matrix(seq_rows):
    c = HG_CHUNK
    t = np.arange(c)[:, None]
    i = np.arange(c)[None, :]
    same = (t // seq_rows) == (i // seq_rows)
    blocks = [same & (i <= t), same & (i > t)]
    for lv in range(1, HG_MATMUL_LEVELS + 1):
        s = 1 << lv
        half = s // 2
        start = (t // s) * s
        upper = (t % s) >= half
        blocks.append(np.where(upper, (i >= start + half) & (i <= t), (i > t) & (i <= start + half - 1)))
    d = np.concatenate(blocks, axis=0).astype(np.float32)
    return np.concatenate([d, d, d], axis=1)


def _hgrn_lower_bound(lbl_ref, layer):
    lg = lbl_ref[...]
    e = jnp.exp(lg - jnp.max(lg, axis=0, keepdims=True))
    sm = e / jnp.sum(e, axis=0, keepdims=True)
    return jnp.sum(sm[1:layer + 1, :], axis=0, keepdims=True)


def _hgrn_gates(cf, lb):
    e = jnp.exp(-jnp.abs(cf))
    inv = 1.0 / (1.0 + e)
    pos = cf >= 0.0
    sig = jnp.where(pos, inv, e * inv)
    sig_neg = jnp.where(pos, e * inv, inv)
    return jnp.log(lb + (1.0 - lb) * sig), (1.0 - lb) * sig_neg


def _hgrn_attention(q, kk, g, dm, levels):
    c = HG_CHUNK
    sums = jnp.dot(dm, jnp.concatenate(_split3(g), axis=0), preferred_element_type=F32)
    b = sums[0:c]
    suffix = sums[c:2 * c]
    row = lax.broadcasted_iota(jnp.int32, (c, HD), 0)
    rr = lax.broadcasted_iota(jnp.int32, (c, c), 0)
    cc = lax.broadcasted_iota(jnp.int32, (c, c), 1)
    att = jnp.where(rr == cc, _dot_nt(q.astype(BF16), kk.astype(BF16)), 0.0)
    for lv in range(1, levels + 1):
        s = 1 << lv
        half = s // 2
        upper = (row & (s - 1)) >= half
        if lv <= HG_MATMUL_LEVELS:
            d = sums[(1 + lv) * c:(2 + lv) * c]
        else:
            mids = [jnp.broadcast_to(b[k * s + half - 1:k * s + half, :], (s, HD)) for k in range(c // s)]
            bmid = mids[0] if len(mids) == 1 else jnp.concatenate(mids, axis=0)
            d = jnp.where(upper, b - bmid, bmid - b)
        e = jnp.exp(d)
        qt = (q * jnp.where(upper, e, 0.0)).astype(BF16)
        kt = (kk * jnp.where(upper, 0.0, e)).astype(BF16)
        att = att + jnp.where((rr >> lv) == (cc >> lv), _dot_nt(qt, kt), 0.0)
    return att, b, suffix


def _hgrn_kernel(q_ref, f_ref, i_ref, g_ref, s0_ref, lbl_ref, gn_ref, dm_ref, y_ref, so_ref, st_s, *, nh, layer):
    c = HG_CHUNK

    @pl.when(pl.program_id(1) == 0)
    def _():
        st_s[...] = s0_ref[...]

    lb = _hgrn_lower_bound(lbl_ref, layer)
    rr = lax.broadcasted_iota(jnp.int32, (c, c), 0)
    cc = lax.broadcasted_iota(jnp.int32, (c, c), 1)
    eye = (rr == cc).astype(F32)
    dm = dm_ref[...]

    for h in range(nh):
        sl = slice(h * HD, (h + 1) * HD)
        q, v = q_ref[:, sl], i_ref[:, sl]
        g, kk = _hgrn_gates(f_ref[:, sl], lb[:, sl])
        att, b, suffix = _hgrn_attention(q, kk, g, dm, c.bit_length() - 1)
        state = st_s[h]
        vb = v.astype(BF16)
        o = jnp.dot((q * jnp.exp(b)).astype(BF16), state.astype(BF16), preferred_element_type=F32)
        o = o + jnp.dot(att.astype(BF16), vb, preferred_element_type=F32)
        kd = kk * jnp.exp(suffix)
        ecol = _col_from_row(jnp.exp(b[c - 1:c, :]), eye)
        st_s[h] = ecol * state + jnp.dot(kd.T.astype(BF16), vb, preferred_element_type=F32)
        gate = g_ref[:, sl]
        y_ref[:, sl] = _rms(o, gn_ref[...]) * (gate * jax.nn.sigmoid(gate))

    so_ref[...] = st_s[...]


def _hgrn_sample_kernel(q_ref, f_ref, i_ref, g_ref, s0_ref, lbl_ref, gn_ref, dm_ref, y_ref, so_ref,
                        qe_s, kdt_s, e_s, o_s, *, t, nh, layer):
    c = HG_CHUNK
    lb = _hgrn_lower_bound(lbl_ref, layer)
    rr = lax.broadcasted_iota(jnp.int32, (c, c), 0)
    cc = lax.broadcasted_iota(jnp.int32, (c, c), 1)
    eye = (rr == cc).astype(F32)
    dm = dm_ref[...]

    for h in range(nh):
        sl = slice(h * HD, (h + 1) * HD)
        q, v = q_ref[:, sl], i_ref[:, sl]
        g, kk = _hgrn_gates(f_ref[:, sl], lb[:, sl])
        att, b, suffix = _hgrn_attention(q, kk, g, dm, t.bit_length() - 1)
        vb = v.astype(BF16)
        e = jnp.exp(b)
        e_s[...] = e
        qe_s[...] = q * e
        kdt_s[...] = (kk * jnp.exp(suffix)).T
        o_s[...] = jnp.dot(att.astype(BF16), vb, preferred_element_type=F32)

        def sequence(n, carry):
            r = pl.multiple_of(n * t, t)
            state = s0_ref[n, h]
            o_s[pl.ds(r, t), :] += jnp.dot(qe_s[pl.ds(r, t), :].astype(BF16), state.astype(BF16),
                                           preferred_element_type=F32)
            ecol = _col_from_row(e_s[pl.ds(r, t), :][t - 1:t, :], eye)
            own = (cc >= r) & (cc < r + t)
            kdt = jnp.where(own, kdt_s[...], 0.0).astype(BF16)
            so_ref[n, h] = ecol * state + jnp.dot(kdt, vb, preferred_element_type=F32)
            return carry

        lax.fori_loop(0, c // t, sequence, 0)
        gate = g_ref[:, sl]
        y_ref[:, sl] = _rms(o_s[...], gn_ref[...]) * (gate * jax.nn.sigmoid(gate))


def hgrn_mixer(z, s0, lb_logits, g_norm, *, n, t, layer):
    nh = s0.shape[1]
    w = nh * HD
    c = HG_CHUNK
    assert t % c == 0
    nc = t // c
    dm = jnp.asarray(_hgrn_sum_matrix(c), BF16)
    col = lambda k: pl.BlockSpec((c, w), lambda i, b: (i * nc + b, k))
    return pl.pallas_call(
        functools.partial(_hgrn_kernel, nh=nh, layer=layer),
        grid=(n, nc),
        in_specs=[
            col(0), col(1), col(2), col(3),
            pl.BlockSpec((None, nh, HD, HD), lambda i, b: (i, 0, 0, 0)),
            pl.BlockSpec(lb_logits.shape, lambda i, b: (0, 0)),
            pl.BlockSpec((1, HD), lambda i, b: (0, 0)),
            pl.BlockSpec(dm.shape, lambda i, b: (0, 0)),
        ],
        out_specs=[
            col(0),
            pl.BlockSpec((None, nh, HD, HD), lambda i, b: (i, 0, 0, 0)),
        ],
        out_shape=[
            jax.ShapeDtypeStruct((n * t, w), F32),
            jax.ShapeDtypeStruct((n, nh, HD, HD), F32),
        ],
        scratch_shapes=[pltpu.VMEM((nh, HD, HD), F32)],
        compiler_params=_params("parallel", "arbitrary"),
        name="hgrn_mixer",
    )(z, z, z, z, s0, lb_logits, g_norm.reshape(1, HD), dm)


def hgrn_sample(z, s0, lb_logits, g_norm, *, s_layer, n, t, layer):
    nh = s0.shape[2]
    w = nh * HD
    c = HG_CHUNK
    assert c % t == 0 and t % 8 == 0 and t & (t - 1) == 0 and (n * t) % c == 0
    nseq = c // t
    dm = jnp.asarray(_hgrn_sum_matrix(t), BF16)
    col = lambda k: pl.BlockSpec((c, w), lambda i: (i, k))
    return pl.pallas_call(
        functools.partial(_hgrn_sample_kernel, t=t, nh=nh, layer=layer),
        grid=(n // nseq,),
        in_specs=[
            col(0), col(1), col(2), col(3),
            pl.BlockSpec((None, nseq, nh, HD, HD), lambda i: (s_layer, i, 0, 0, 0)),
            pl.BlockSpec(lb_logits.shape, lambda i: (0, 0)),
            pl.BlockSpec((1, HD), lambda i: (0, 0)),
            pl.BlockSpec(dm.shape, lambda i: (0, 0)),
        ],
        out_specs=[
            col(0),
            pl.BlockSpec((nseq, nh, HD, HD), lambda i: (i, 0, 0, 0)),
        ],
        out_shape=[
            jax.ShapeDtypeStruct((n * t, w), F32),
            jax.ShapeDtypeStruct((n, nh, HD, HD), F32),
        ],
        scratch_shapes=[pltpu.VMEM((c, HD), F32)] * 4,
        compiler_params=_params("parallel"),
        name="hgrn_sample",
    )(z, z, z, z, s0, lb_logits, g_norm.reshape(1, HD), dm)


def _merge_groups(outs, lses):
    mx = functools.reduce(jnp.maximum, lses)
    ws = [jnp.exp(l - mx) for l in lses]
    den = functools.reduce(lambda a, b: a + b, ws)
    return functools.reduce(lambda a, b: a + b, [w * o for w, o in zip(ws, outs)]) / den


DSW_BLOCK = 128


def _dsw_prompt_kernel(q0_ref, q1_ref, q2_ref, k_ref, v_ref, o_ref, og_s, lse_s, *, s_len, scale):
    qrefs = (q0_ref, q1_ref, q2_ref)
    qb = DSW_BLOCK
    i = lax.broadcasted_iota(jnp.int32, (qb, 2 * qb), 0)
    j = lax.broadcasted_iota(jnp.int32, (qb, 2 * qb), 1)
    for g, (win, dil) in enumerate(D_GROUPS):
        band = win // dil
        assert band <= qb and (s_len // dil) % qb == 0
        band_mask = (j >= i + qb - band) & (j <= i + qb)
        first_mask = band_mask[:, qb:]
        for r in range(dil):
            for lb in range(s_len // dil // qb):
                rows = pl.ds(r + lb * qb * dil, qb, stride=dil)
                q = qrefs[g][rows, :].astype(BF16)
                if lb == 0:
                    keys, mask = rows, first_mask
                else:
                    keys, mask = pl.ds(r + (lb - 1) * qb * dil, 2 * qb, stride=dil), band_mask
                s = jnp.where(mask, _dot_nt(q, k_ref[keys, :].astype(BF16)) * scale, -jnp.inf)
                m = jnp.max(s, axis=-1, keepdims=True)
                p = jnp.exp(s - m)
                l = jnp.sum(p, axis=-1, keepdims=True)
                o = jnp.dot(p.astype(BF16), v_ref[keys, :].astype(BF16), preferred_element_type=F32)
                og_s[g, rows, :] = o / l
                lse_s[g, rows, :] = jnp.broadcast_to(m + jnp.log(l), (qb, HD))
    ng = len(D_GROUPS)
    o_ref[...] = _merge_groups([og_s[g] for g in range(ng)], [lse_s[g] for g in range(ng)])


def dsw_prompt(z, *, n, s_len, nkv, qcol):
    ng = len(D_GROUPS)
    c0 = qcol * nkv
    spec = lambda k: pl.BlockSpec((s_len, HD), lambda b, h: (b, c0 + k * nkv + h))
    return pl.pallas_call(
        functools.partial(_dsw_prompt_kernel, s_len=s_len, scale=HD ** -0.5),
        grid=(n, nkv),
        in_specs=[spec(0), spec(1), spec(2), spec(ng), spec(ng + 1)],
        out_specs=pl.BlockSpec((s_len, HD), lambda b, h: (b, h)),
        out_shape=jax.ShapeDtypeStruct((n * s_len, nkv * HD), F32),
        scratch_shapes=[pltpu.VMEM((ng, s_len, HD), F32), pltpu.VMEM((ng, s_len, HD), F32)],
        compiler_params=_params("parallel", "parallel"),
        name="dsw_prompt",
    )(z, z, z, z, z)


def _dsw_sample_kernel(q0_ref, q1_ref, q2_ref, kn_ref, vn_ref, kc_ref, vc_ref, o_ref, kn_s, vn_s,
                       *, t, nkv, wbuf, scale):
    ng = len(D_GROUPS)
    kn_s[...] = jnp.zeros_like(kn_s)
    vn_s[...] = jnp.zeros_like(vn_s)
    kn_s[0:t, :] = kn_ref[...]
    vn_s[0:t, :] = vn_ref[...]

    def group_consts(shape):
        r = lax.broadcasted_iota(jnp.int32, shape, 0)
        grp = r // t
        win = jnp.zeros(shape, jnp.int32)
        dmask = jnp.zeros(shape, jnp.int32)
        for g, (w_, d_) in enumerate(D_GROUPS):
            win = jnp.where(grp == g, w_, win)
            dmask = jnp.where(grp == g, d_ - 1, dmask)
        return r % t, win, dmask

    tok_c, win_c, dm_c = group_consts((ng * t, wbuf))
    dist_c = wbuf + tok_c - lax.broadcasted_iota(jnp.int32, (ng * t, wbuf), 1)
    valid_c = (dist_c >= 0) & (dist_c <= win_c) & ((dist_c & dm_c) == 0)
    tok_n, win_n, dm_n = group_consts((ng * t, HD))
    dist_n = tok_n - lax.broadcasted_iota(jnp.int32, (ng * t, HD), 1)
    valid_n = (dist_n >= 0) & (dist_n <= win_n) & ((dist_n & dm_n) == 0)

    for h in range(nkv):
        sl = slice(h * HD, (h + 1) * HD)
        q = jnp.concatenate([q0_ref[:, sl], q1_ref[:, sl], q2_ref[:, sl]], axis=0).astype(BF16)
        kc = kc_ref[pl.ds(h, wbuf, stride=nkv), :].astype(BF16)
        vc = vc_ref[pl.ds(h, wbuf, stride=nkv), :].astype(BF16)
        s_c = jnp.where(valid_c, _dot_nt(q, kc) * scale, -jnp.inf)
        s_n = jnp.where(valid_n, _dot_nt(q, kn_s[:, sl].astype(BF16)) * scale, -jnp.inf)
        m = jnp.maximum(jnp.max(s_c, axis=-1, keepdims=True), jnp.max(s_n, axis=-1, keepdims=True))
        p_c = jnp.exp(s_c - m)
        p_n = jnp.exp(s_n - m)
        l = jnp.sum(p_c, axis=-1, keepdims=True) + jnp.sum(p_n, axis=-1, keepdims=True)
        o = jnp.dot(p_c.astype(BF16), vc, preferred_element_type=F32)
        o = (o + jnp.dot(p_n.astype(BF16), vn_s[:, sl].astype(BF16), preferred_element_type=F32)) / l
        lse = m + jnp.log(l)
        o_ref[:, sl] = _merge_groups([o[g * t:(g + 1) * t] for g in range(ng)],
                                     [lse[g * t:(g + 1) * t] for g in range(ng)])


def dsw_sample(z, cache_k, cache_v, *, layer, n, t, nkv, qcol, row0):
    w = nkv * HD
    n_layers, _, wbuf = cache_k.shape[:3]
    cache_k = cache_k.reshape(n_layers, n, wbuf * nkv, HD)
    cache_v = cache_v.reshape(n_layers, n, wbuf * nkv, HD)
    ng = len(D_GROUPS)
    rb0 = row0 // t
    col = lambda k: pl.BlockSpec((t, w), lambda i: (rb0 + i, qcol + k))
    cache = pl.BlockSpec((None, None, wbuf * nkv, HD), lambda i: (layer, i, 0, 0))
    return pl.pallas_call(
        functools.partial(_dsw_sample_kernel, t=t, nkv=nkv, wbuf=wbuf, scale=HD ** -0.5),
        grid=(n,),
        in_specs=[col(0), col(1), col(2), col(ng), col(ng + 1), cache, cache],
        out_specs=pl.BlockSpec((t, w), lambda i: (i, 0)),
        out_shape=jax.ShapeDtypeStruct((n * t, w), F32),
        scratch_shapes=[pltpu.VMEM((HD, w), F32), pltpu.VMEM((HD, w), F32)],
        compiler_params=_params("parallel"),
        name="dsw_sample",
    )(z, z, z, z, z, cache_k, cache_v)


def kernel(x_prompt, x_sample, cache_fox_k, cache_fox_v, cache_fox_logf, page_table, state_lru_conv, state_lru_h, state_hgrn, cache_dsw_k, cache_dsw_v, norm_gains, ffn_w_gate, ffn_w_up, ffn_w_down, even_w_in, even_w_out, lru_conv_w, lru_conv_b, lru_w_r, lru_b_r, lru_w_i, lru_b_i, lru_lambda, fox_b_f, odd_w_in, odd_w_out, hgrn_lb_logits, hgrn_norm, final_norm):
    n_p, s_len, d = x_prompt.shape
    n_s, t_s, _ = x_sample.shape
    depth = norm_gains.shape[0]
    a_w = lru_lambda.shape[-1]
    b_heads = fox_b_f.shape[-1]
    b_w = b_heads * HD
    c_heads = state_hgrn.shape[2]
    c_w = c_heads * HD
    d_heads = cache_dsw_k.shape[3]
    d_w = d_heads * HD
    n_pool = cache_fox_k.shape[1]
    wbuf = cache_dsw_k.shape[2]
    keep = min(max(w for w, _ in D_GROUPS), s_len)

    xp = x_prompt.reshape(n_p * s_len, d)
    xs = x_sample.reshape(n_s * t_s, d)
    tm_p, tm_s = 1024, n_s * t_s
    names = ('fox_k', 'fox_v', 'fox_logf', 'lru_conv', 'lru_h', 'hgrn', 'dsw_k', 'dsw_v')
    st_p = {nm: [] for nm in names}
    st_s = {nm: [] for nm in names}

    def ffn_both(xp, xs, l, k, final):
        g = norm_gains[l, 2 * k]
        w3 = (ffn_w_gate, ffn_w_up, ffn_w_down)
        return (ffn(xp, g, *w3, final_norm, l=l, k=k, final=final, tm=tm_p, tf=256),
                ffn(xs, g, *w3, final_norm, l=l, k=k, final=final, tm=tm_s, tf=512))

    for l in range(depth):
        j = l // 2
        xp, xs = ffn_both(xp, xs, l, 0, False)
        g_mix = norm_gains[l, 1]
        if l % 2 == 0:
            n_main = 2 * a_w + 3 * b_w
            w_in = even_w_in[j]
            w_f = jnp.pad(w_in[:, n_main:], ((0, 0), (0, HD - b_heads)))
            bfp = jnp.pad(fox_b_f[j], (0, HD - b_heads)).reshape(1, HD)
            wri = jnp.concatenate([lru_w_r[j], lru_w_i[j]], axis=-1).astype(BF16)
            w_out = even_w_out[j]
            lru_args = (lru_conv_w[j], lru_conv_b[j], wri, lru_b_r[j], lru_b_i[j], lru_lambda[j])
            qcol = 2 * a_w // b_w

            zp = norm_matmul(xp, g_mix, w_in, tm=tm_p, tn=512)
            zfp = norm_matmul(xp, g_mix, w_f, tm=tm_p, tn=HD)
            zs = norm_matmul(xs, g_mix, w_in, tm=tm_s, tn=512)
            zfs = norm_matmul(xs, g_mix, w_f, tm=tm_s, tn=HD)

            ya_p, cv_p, h_p = lru_mixer(zp, jnp.zeros((n_p, CONV_W - 1, a_w), F32), jnp.zeros((n_p, a_w), F32),
                                        *lru_args, n=n_p, t=s_len, tt=256, starts=True)
            lf_p, cum_p, cumt_p = fox_prep(zfp, bfp, n=n_p, s_len=s_len, nh=b_heads)
            ob_p = fox_prompt(zp, cum_p, cumt_p, n=n_p, s_len=s_len, tq=512, qcol=qcol)

            ya_s, cv_s, h_s = lru_mixer(zs, state_lru_conv[j], state_lru_h[j],
                                        *lru_args, n=n_s, t=t_s, tt=t_s, starts=False)
            lf_s, _, _ = fox_prep(zfs, bfp, n=1, s_len=n_s * t_s, nh=b_heads)
            k_s = zs[:, 2 * a_w + b_w:2 * a_w + 2 * b_w]
            v_s = zs[:, 2 * a_w + 2 * b_w:2 * a_w + 3 * b_w]
            pad_rows = lambda a: jnp.pad(a.reshape(n_s, t_s, -1), ((0, 0), (0, HD - t_s), (0, 0)))
            ob_s = fox_sample(page_table, zs, pad_rows(k_s), pad_rows(v_s), pad_rows(lf_s),
                              cache_fox_k, cache_fox_v, cache_fox_logf,
                              layer=j, n=n_s, t=t_s, nh=b_heads, qcol=qcol, pp=8)

            xp = out_proj(xp, ya_p, ob_p, w_out, tm=tm_p, tn=512)
            xs = out_proj(xs, ya_s, ob_s, w_out, tm=tm_s, tn=512)

            k_p = zp[:, 2 * a_w + b_w:2 * a_w + 2 * b_w]
            v_p = zp[:, 2 * a_w + 2 * b_w:2 * a_w + 3 * b_w]
            st_p['fox_k'].append(k_p.reshape(n_p, s_len, b_heads, HD))
            st_p['fox_v'].append(v_p.reshape(n_p, s_len, b_heads, HD))
            st_p['fox_logf'].append(lf_p.reshape(n_p, s_len, b_heads))
            st_p['lru_conv'].append(cv_p)
            st_p['lru_h'].append(h_p.reshape(n_p, a_w))
            st_s['fox_k'].append(k_s.reshape(n_s, t_s, b_heads, HD))
            st_s['fox_v'].append(v_s.reshape(n_s, t_s, b_heads, HD))
            st_s['fox_logf'].append(lf_s.reshape(n_s, t_s, b_heads))
            st_s['lru_conv'].append(cv_s)
            st_s['lru_h'].append(h_s.reshape(n_s, a_w))
        else:
            w_in = odd_w_in[j]
            w_out = odd_w_out[j]
            qcol = 4 * c_w // d_w
            zp = norm_matmul(xp, g_mix, w_in, tm=tm_p, tn=512)
            zs = norm_matmul(xs, g_mix, w_in, tm=tm_s, tn=512)

            oc_p, s_p = hgrn_mixer(zp, jnp.zeros((n_p, c_heads, HD, HD), F32), hgrn_lb_logits, hgrn_norm[j],
                                   n=n_p, t=s_len, layer=l)
            od_p = dsw_prompt(zp, n=n_p, s_len=s_len, nkv=d_heads, qcol=qcol)
            oc_s, s_s = hgrn_sample(zs, state_hgrn, hgrn_lb_logits, hgrn_norm[j],
                                    s_layer=j, n=n_s, t=t_s, layer=l)
            od_s = dsw_sample(zs, cache_dsw_k, cache_dsw_v,
                              layer=j, n=n_s, t=t_s, nkv=d_heads, qcol=qcol, row0=0)

            xp = out_proj(xp, oc_p, od_p, w_out, tm=tm_p, tn=512)
            xs = out_proj(xs, oc_s, od_s, w_out, tm=tm_s, tn=512)

            k0 = 4 * c_w + len(D_GROUPS) * d_w
            zp3 = zp.reshape(n_p, s_len, -1)
            st_p['hgrn'].append(s_p)
            st_p['dsw_k'].append(zp3[:, s_len - keep:, k0:k0 + d_w].reshape(n_p, keep, d_heads, HD))
            st_p['dsw_v'].append(zp3[:, s_len - keep:, k0 + d_w:k0 + 2 * d_w].reshape(n_p, keep, d_heads, HD))
            st_s['hgrn'].append(s_s)
            st_s['dsw_k'].append(zs[:, k0:k0 + d_w].reshape(n_s, t_s, d_heads, HD))
            st_s['dsw_v'].append(zs[:, k0 + d_w:k0 + 2 * d_w].reshape(n_s, t_s, d_heads, HD))
        xp, xs = ffn_both(xp, xs, l, 1, l == depth - 1)

    y_prompt = xp.reshape(n_p, s_len, d)
    y_sample = xs.reshape(n_s, t_s, d)
    return (y_prompt, y_sample,
            *(jnp.stack(st_p[nm]) for nm in names),
            *(jnp.stack(st_s[nm]) for nm in names))
```

```python
import functools

import numpy as np
import jax
import jax.numpy as jnp
from jax import lax
from jax.experimental import pallas as pl
from jax.experimental.pallas import tpu as pltpu

F32 = jnp.float32
BF16 = jnp.bfloat16
EPS = 1e-6
HD = 128
LRU_C = 8.0
CONV_W = 4
D_GROUPS = ((128, 1), (512, 4), (2048, 16))
VMEM_LIMIT_BYTES = 58 * 1024 * 1024
HIGHEST = lax.Precision.HIGHEST
NT_DIMS = (((1,), (1,)), ((), ()))


def _params(*sem):
    return pltpu.CompilerParams(dimension_semantics=sem, vmem_limit_bytes=VMEM_LIMIT_BYTES)


def _rms(x, g):
    ms = jnp.mean(x * x, axis=-1, keepdims=True)
    return x * lax.rsqrt(ms + EPS) * g


def _softplus(z):
    return jnp.maximum(z, 0.0) + jnp.log1p(jnp.exp(-jnp.abs(z)))


def _log_sigmoid(z):
    return -_softplus(-z)


def _dot_nt(a, b):
    return lax.dot_general(a, b, NT_DIMS, preferred_element_type=F32)


def _col_from_row(row, eye):
    return jnp.sum(eye * row, axis=1, keepdims=True)


def _ffn_kernel(x_ref, g_ref, wg_ref, wu_ref, wd_ref, gf_ref, o_ref, xn_ref, *, nf, final):
    f = pl.program_id(1)

    @pl.when(f == 0)
    def _():
        xn_ref[...] = _rms(x_ref[...], g_ref[...]).astype(BF16)
        o_ref[...] = jnp.zeros_like(o_ref)

    xn = xn_ref[...]
    a = jnp.dot(xn, wg_ref[...].astype(BF16), preferred_element_type=F32)
    b = jnp.dot(xn, wu_ref[...].astype(BF16), preferred_element_type=F32)
    h = (a * jax.nn.sigmoid(a) * b).astype(BF16)
    o_ref[...] += jnp.dot(h, wd_ref[...].astype(BF16), preferred_element_type=F32)

    @pl.when(f == nf - 1)
    def _():
        y = x_ref[...] + 0.5 * o_ref[...]
        if final:
            y = _rms(y, gf_ref[...])
        o_ref[...] = y


def ffn(x, g, wg, wu, wd, gf, *, l, k, final, tm, tf):
    m, d = x.shape
    nf = wg.shape[-1] // tf
    return pl.pallas_call(
        functools.partial(_ffn_kernel, nf=nf, final=final),
        grid=(m // tm, nf),
        in_specs=[
            pl.BlockSpec((tm, d), lambda i, f: (i, 0)),
            pl.BlockSpec((1, d), lambda i, f: (0, 0)),
            pl.BlockSpec((None, None, d, tf), lambda i, f: (l, k, 0, f)),
            pl.BlockSpec((None, None, d, tf), lambda i, f: (l, k, 0, f)),
            pl.BlockSpec((None, None, tf, d), lambda i, f: (l, k, f, 0)),
            pl.BlockSpec((1, d), lambda i, f: (0, 0)),
        ],
        out_specs=pl.BlockSpec((tm, d), lambda i, f: (i, 0)),
        out_shape=jax.ShapeDtypeStruct((m, d), F32),
        scratch_shapes=[pltpu.VMEM((tm, d), BF16)],
        compiler_params=_params("parallel", "arbitrary"),
        name="ffn",
    )(x, g.reshape(1, d), wg, wu, wd, gf.reshape(1, d))


def _norm_matmul_kernel(x_ref, g_ref, w_ref, o_ref, xn_ref):
    @pl.when(pl.program_id(1) == 0)
    def _():
        xn_ref[...] = _rms(x_ref[...], g_ref[...]).astype(BF16)

    o_ref[...] = jnp.dot(xn_ref[...], w_ref[...].astype(BF16), preferred_element_type=F32)


def norm_matmul(x, g, w, *, tm, tn):
    m, d = x.shape
    n = w.shape[1] // tn * tn
    return pl.pallas_call(
        _norm_matmul_kernel,
        grid=(m // tm, n // tn),
        in_specs=[
            pl.BlockSpec((tm, d), lambda i, j: (i, 0)),
            pl.BlockSpec((1, d), lambda i, j: (0, 0)),
            pl.BlockSpec((d, tn), lambda i, j: (0, j)),
        ],
        out_specs=pl.BlockSpec((tm, tn), lambda i, j: (i, j)),
        out_shape=jax.ShapeDtypeStruct((m, n), F32),
        scratch_shapes=[pltpu.VMEM((tm, d), BF16)],
        compiler_params=_params("parallel", "arbitrary"),
        name="norm_matmul",
    )(x, g.reshape(1, d), w)


def _out_proj_kernel(r_ref, ya_ref, yb_ref, wa_ref, wb_ref, o_ref, ya_s, yb_s):
    @pl.when(pl.program_id(1) == 0)
    def _():
        ya_s[...] = ya_ref[...].astype(BF16)
        yb_s[...] = yb_ref[...].astype(BF16)

    acc = jnp.dot(ya_s[...], wa_ref[...].astype(BF16), preferred_element_type=F32)
    acc = acc + jnp.dot(yb_s[...], wb_ref[...].astype(BF16), preferred_element_type=F32)
    o_ref[...] = r_ref[...] + acc


def out_proj(res, ya, yb, w, *, tm, tn):
    m, d = res.shape
    ka, kb = ya.shape[1], yb.shape[1]
    assert ka % kb == 0
    return pl.pallas_call(
        _out_proj_kernel,
        grid=(m // tm, d // tn),
        in_specs=[
            pl.BlockSpec((tm, tn), lambda i, j: (i, j)),
            pl.BlockSpec((tm, ka), lambda i, j: (i, 0)),
            pl.BlockSpec((tm, kb), lambda i, j: (i, 0)),
            pl.BlockSpec((ka, tn), lambda i, j: (0, j)),
            pl.BlockSpec((kb, tn), lambda i, j: (ka // kb, j)),
        ],
        out_specs=pl.BlockSpec((tm, tn), lambda i, j: (i, j)),
        out_shape=jax.ShapeDtypeStruct((m, d), F32),
        scratch_shapes=[pltpu.VMEM((tm, ka), BF16), pltpu.VMEM((tm, kb), BF16)],
        compiler_params=_params("parallel", "arbitrary"),
        name="out_proj",
    )(res, ya, yb, w, w)


def _lru_kernel(ax_ref, ag_ref, cb0_ref, h0_ref, cw_ref, cb_ref, wri_ref, br_ref, bi_ref, lam_ref,
                ya_ref, cst_ref, hl_ref, xbuf, hcar, a_s, b_s, *, tt, nh, starts):
    tb = pl.program_id(1)

    @pl.when(tb == 0)
    def _():
        xbuf[5:8, :] = cb0_ref[...]
        hcar[...] = h0_ref[...]

    x = ax_ref[...]
    xbuf[8:8 + tt, :] = x
    cw = cw_ref[...]
    y = cb_ref[...] + xbuf[5:5 + tt, :] * cw[0:1, :]
    y = y + xbuf[6:6 + tt, :] * cw[1:2, :]
    y = y + xbuf[7:7 + tt, :] * cw[2:3, :]
    y = y + x * cw[3:4, :]
    tail = xbuf[5 + tt:8 + tt, :]
    xbuf[5:8, :] = tail
    cst_ref[...] = tail

    sp = _softplus(-lam_ref[...])
    for h in range(nh):
        sl = slice(h * HD, (h + 1) * HD)
        yh = y[:, sl]
        gts = jnp.dot(yh.astype(BF16), wri_ref[h], preferred_element_type=F32)
        gr = jax.nn.sigmoid(gts[:, :HD] + br_ref[:, sl])
        gi = jax.nn.sigmoid(gts[:, HD:] + bi_ref[:, sl])
        log_a = -LRU_C * gr * sp[:, sl]
        a = jnp.exp(log_a)
        mult = jnp.sqrt(-jnp.tanh(log_a) * (a * a + 1.0))
        if starts:
            row = lax.broadcasted_iota(jnp.int32, mult.shape, 0)
            mult = jnp.where((row == 0) & (tb == 0), 1.0, mult)
        a_s[:, sl] = a
        b_s[:, sl] = mult * gi * yh

    def group(gidx, hprev):
        r = pl.multiple_of(gidx * 8, 8)
        a8 = a_s[pl.ds(r, 8), :]
        b8 = b_s[pl.ds(r, 8), :]
        row = lax.broadcasted_iota(jnp.int32, a8.shape, 0)
        for s in (1, 2, 4):
            ok = row >= s
            b8 = jnp.where(ok, a8 * pltpu.roll(b8, s, 0) + b8, b8)
            a8 = jnp.where(ok, a8 * pltpu.roll(a8, s, 0), a8)
        h8 = a8 * hprev + b8
        b_s[pl.ds(r, 8), :] = h8
        return h8[7:8, :]

    hlast = lax.fori_loop(0, tt // 8, group, hcar[...])
    hcar[...] = hlast
    hl_ref[...] = hlast
    ya_ref[...] = b_s[...] * jax.nn.gelu(ag_ref[...])


def lru_mixer(z, conv0, h0, cw, cb, wri, br, bi, lam, *, n, t, tt, starts):
    w = lam.shape[-1]
    nh = w // HD
    nb = t // tt
    row = lambda i, b: (i * nb + b, 0)
    vec = pl.BlockSpec((1, w), lambda i, b: (0, 0))
    return pl.pallas_call(
        functools.partial(_lru_kernel, tt=tt, nh=nh, starts=starts),
        grid=(n, nb),
        in_specs=[
            pl.BlockSpec((tt, w), row),
            pl.BlockSpec((tt, w), lambda i, b: (i * nb + b, 1)),
            pl.BlockSpec((None, CONV_W - 1, w), lambda i, b: (i, 0, 0)),
            pl.BlockSpec((None, 1, w), lambda i, b: (i, 0, 0)),
            pl.BlockSpec((CONV_W, w), lambda i, b: (0, 0)),
            vec,
            pl.BlockSpec((nh, HD, 2 * HD), lambda i, b: (0, 0, 0)),
            vec, vec, vec,
        ],
        out_specs=[
            pl.BlockSpec((tt, w), row),
            pl.BlockSpec((None, CONV_W - 1, w), lambda i, b: (i, 0, 0)),
            pl.BlockSpec((None, 1, w), lambda i, b: (i, 0, 0)),
        ],
        out_shape=[
            jax.ShapeDtypeStruct((n * t, w), F32),
            jax.ShapeDtypeStruct((n, CONV_W - 1, w), F32),
            jax.ShapeDtypeStruct((n, 1, w), F32),
        ],
        scratch_shapes=[
            pltpu.VMEM((tt + 8, w), F32),
            pltpu.VMEM((1, w), F32),
            pltpu.VMEM((tt, w), F32),
            pltpu.VMEM((tt, w), F32),
        ],
        compiler_params=_params("parallel", "arbitrary"),
        name="lru_mixer",
    )(z, z, conv0, h0.reshape(n, 1, w), cw, cb.reshape(1, w), wri,
      br.reshape(1, w), bi.reshape(1, w), lam.reshape(1, w))


def _fox_prep_kernel(zf_ref, bf_ref, lf_ref, cum_ref, cumt_ref, cum_s, cumt_s, *, s_len, nh):
    lf = _log_sigmoid(zf_ref[...] + bf_ref[...])
    lf_ref[...] = lf[:, :nh]
    cum_s[...] = lf
    r = lax.broadcasted_iota(jnp.int32, (HD, HD), 0)
    c = lax.broadcasted_iota(jnp.int32, (HD, HD), 1)
    tri = (r >= c).astype(F32)

    def chunk(i, carry):
        st = pl.multiple_of(i * HD, HD)
        cs = jnp.dot(tri, cum_s[pl.ds(st, HD), :], preferred_element_type=F32, precision=HIGHEST) + carry
        cum_s[pl.ds(st, HD), :] = cs
        cumt_s[:, pl.ds(st, HD)] = cs.T
        return cs[HD - 1:HD, :]

    lax.fori_loop(0, s_len // HD, chunk, jnp.zeros((1, HD), F32))
    cum_ref[...] = cum_s[:, :nh]
    cumt_ref[...] = cumt_s[:nh, :]


def fox_prep(zf, bfp, *, n, s_len, nh):
    return pl.pallas_call(
        functools.partial(_fox_prep_kernel, s_len=s_len, nh=nh),
        grid=(n,),
        in_specs=[
            pl.BlockSpec((s_len, HD), lambda i: (i, 0)),
            pl.BlockSpec((1, HD), lambda i: (0, 0)),
        ],
        out_specs=[
            pl.BlockSpec((s_len, nh), lambda i: (i, 0)),
            pl.BlockSpec((s_len, nh), lambda i: (i, 0)),
            pl.BlockSpec((None, nh, s_len), lambda i: (i, 0, 0)),
        ],
        out_shape=[
            jax.ShapeDtypeStruct((n * s_len, nh), F32),
            jax.ShapeDtypeStruct((n * s_len, nh), F32),
            jax.ShapeDtypeStruct((n, nh, s_len), F32),
        ],
        scratch_shapes=[pltpu.VMEM((s_len, HD), F32), pltpu.VMEM((HD, s_len), F32)],
        compiler_params=_params("parallel"),
        name="fox_prep",
    )(zf, bfp)


LOG2E = 1.4426950408889634


def _fox_prompt_kernel(q_ref, k_ref, v_ref, cq_ref, ck_ref, o_ref, m_s, l_s, acc_s, *, tq, nh, scale):
    qi = pl.program_id(1)
    kj = pl.program_id(2)

    @pl.when(kj == 0)
    def _():
        m_s[...] = jnp.full_like(m_s, -jnp.inf)
        l_s[...] = jnp.zeros_like(l_s)
        acc_s[...] = jnp.zeros_like(acc_s)

    def block(causal):
        for h in range(nh):
            sl = slice(h * HD, (h + 1) * HD)
            u = _dot_nt(q_ref[:, sl].astype(BF16), k_ref[:, sl].astype(BF16)) * (scale * LOG2E)
            u = u - ck_ref[h:h + 1, :] * LOG2E
            if causal is not None:
                u = jnp.where(causal, u, -jnp.inf)
            cq = cq_ref[:, h:h + 1] * LOG2E
            m_prev = m_s[h]
            m_new = jnp.maximum(m_prev, jnp.max(u, axis=-1, keepdims=True) + cq)
            alpha = jnp.exp2(m_prev - m_new)
            p = jnp.exp2(u - (m_new - cq))
            l_s[h] = alpha * l_s[h] + jnp.sum(p, axis=-1, keepdims=True)
            acc_s[:, sl] = alpha * acc_s[:, sl] + jnp.dot(
                p.astype(BF16), v_ref[:, sl].astype(BF16), preferred_element_type=F32)
            m_s[h] = m_new

    @pl.when(kj < qi)
    def _():
        block(None)

    @pl.when(kj == qi)
    def _():
        row = lax.broadcasted_iota(jnp.int32, (tq, tq), 0)
        col = lax.broadcasted_iota(jnp.int32, (tq, tq), 1)
        block(col <= row)
        for h in range(nh):
            sl = slice(h * HD, (h + 1) * HD)
            o_ref[:, sl] = acc_s[:, sl] / l_s[h]


def fox_prompt(z, cum, cumt, *, n, s_len, tq, qcol):
    nh = cum.shape[1]
    w = nh * HD
    nq = s_len // tq
    return pl.pallas_call(
        functools.partial(_fox_prompt_kernel, tq=tq, nh=nh, scale=HD ** -0.5),
        grid=(n, nq, nq),
        in_specs=[
            pl.BlockSpec((tq, w), lambda i, a, b: (i * nq + a, qcol)),
            pl.BlockSpec((tq, w), lambda i, a, b: (i * nq + jnp.minimum(a, b), qcol + 1)),
            pl.BlockSpec((tq, w), lambda i, a, b: (i * nq + jnp.minimum(a, b), qcol + 2)),
            pl.BlockSpec((tq, nh), lambda i, a, b: (i * nq + a, 0)),
            pl.BlockSpec((None, nh, tq), lambda i, a, b: (i, 0, jnp.minimum(a, b))),
        ],
        out_specs=pl.BlockSpec((tq, w), lambda i, a, b: (i * nq + a, 0)),
        out_shape=jax.ShapeDtypeStruct((n * s_len, w), F32),
        scratch_shapes=[
            pltpu.VMEM((nh, tq, 1), F32),
            pltpu.VMEM((nh, tq, 1), F32),
            pltpu.VMEM((tq, w), F32),
        ],
        compiler_params=_params("parallel", "parallel", "arbitrary"),
        name="fox_prompt",
    )(z, z, z, cum, cumt)


def _split3(x):
    hi = x.astype(BF16)
    r1 = x - hi.astype(F32)
    mid = r1.astype(BF16)
    return hi, mid, (r1 - mid.astype(F32)).astype(BF16)


def _fox_sample_kernel(pt_ref, q_ref, kn_ref, vn_ref, lfn_ref, *refs, pp, t, nh, scale):
    kc, vc, lc = refs[0:pp], refs[pp:2 * pp], refs[2 * pp:3 * pp]
    o_ref = refs[3 * pp]
    qb_s, crow_s, carry_s, m_s, l_s, acc_s = refs[3 * pp + 1:]
    j = pl.program_id(1)
    nj = pl.num_programs(1)
    ncol = nh * t

    r128 = lax.broadcasted_iota(jnp.int32, (HD, HD), 0)
    c128 = lax.broadcasted_iota(jnp.int32, (HD, HD), 1)
    eye = (r128 == c128).astype(F32)
    later_in_page = (c128 > r128).astype(BF16)
    hrow = lax.broadcasted_iota(jnp.int32, (nh, HD), 0)
    hcol = lax.broadcasted_iota(jnp.int32, (nh, HD), 1)
    expand = ((hcol // t == hrow) & (hcol < ncol)).astype(BF16)

    def to_columns(lf):
        return [jnp.dot(part, expand, preferred_element_type=F32).astype(BF16) for part in _split3(lf)]

    def block(k2d, v2d, lf, valid):
        parts = jnp.concatenate(to_columns(lf), axis=1)
        carry = carry_s[...]
        crow = crow_s[...]
        later = []
        for i in reversed(range(k2d.shape[0] // HD)):
            page = parts[i * HD:(i + 1) * HD]
            within = jnp.dot(later_in_page, page, preferred_element_type=F32)
            later.append(within[:, :HD] + within[:, HD:2 * HD] + within[:, 2 * HD:] + (carry + crow))
            tot = jnp.sum(page.astype(F32), axis=0, keepdims=True)
            carry = carry + (tot[:, :HD] + tot[:, HD:2 * HD] + tot[:, 2 * HD:])
        carry_s[...] = carry
        later = later[0] if len(later) == 1 else jnp.concatenate(later[::-1], axis=0)
        s = _dot_nt(k2d, qb_s[...]) * scale + later
        if valid is not None:
            s = jnp.where(valid, s, -jnp.inf)
        m_prev = m_s[...]
        m_new = jnp.maximum(m_prev, jnp.max(s, axis=0, keepdims=True))
        alpha = jnp.exp(m_prev - m_new)
        p = jnp.exp(s - m_new)
        l_s[...] = alpha * l_s[...] + jnp.sum(p, axis=0, keepdims=True)
        m_s[...] = m_new
        acc_s[...] = _col_from_row(alpha, eye)[:ncol] * acc_s[...] + jnp.dot(
            p.T[:ncol].astype(BF16), v2d, preferred_element_type=F32)

    def heads_to_lanes(ref):
        return jnp.concatenate([ref[pl.ds(h, HD, stride=nh), :] for h in range(nh)], axis=1).astype(BF16)

    @pl.when(j == 0)
    def _():
        q = q_ref[...]
        w = q.shape[1]
        qrep = jnp.concatenate([q] * nh + [jnp.zeros((HD - ncol, w), F32)], axis=0)
        rr = lax.broadcasted_iota(jnp.int32, (HD, w), 0)
        cc = lax.broadcasted_iota(jnp.int32, (HD, w), 1)
        qb_s[...] = jnp.where(rr // t == cc // HD, qrep, 0.0).astype(BF16)
        m_s[...] = jnp.full_like(m_s, -jnp.inf)
        l_s[...] = jnp.zeros_like(l_s)
        acc_s[...] = jnp.zeros_like(acc_s)
        carry_s[...] = jnp.zeros_like(carry_s)
        lfn = lfn_ref[...]
        lfn_cols = functools.reduce(lambda a, b: a + b, [p.astype(F32) for p in to_columns(lfn)])
        after = r128 > (c128 % t)
        crow_s[...] = -jnp.sum(jnp.where(after, lfn_cols, 0.0), axis=0, keepdims=True)
        block(kn_ref[...].astype(BF16), vn_ref[...].astype(BF16), lfn, r128 <= (c128 % t))

    block(jnp.concatenate([heads_to_lanes(r) for r in kc], axis=0),
          jnp.concatenate([heads_to_lanes(r) for r in vc], axis=0),
          jnp.concatenate([r[...] for r in lc], axis=0), None)

    @pl.when(j == nj - 1)
    def _():
        lcol = _col_from_row(l_s[...], eye)
        for h in range(nh):
            o_ref[:, h * HD:(h + 1) * HD] = (
                acc_s[h * t:(h + 1) * t, h * HD:(h + 1) * HD] / lcol[h * t:(h + 1) * t, :])


def fox_sample(page_table, zs, kn_pad, vn_pad, lfn_pad, cache_k, cache_v, cache_lf, *, layer, n, t, nh, qcol, pp):
    w = nh * HD
    n_layers, n_pool = cache_k.shape[:2]
    cache_k = cache_k.reshape(n_layers, n_pool, HD * nh, HD)
    cache_v = cache_v.reshape(n_layers, n_pool, HD * nh, HD)
    n_pages = page_table.shape[1]
    nj = n_pages // pp

    def pg(i, tail):
        return lambda b, j, pt: (layer, pt[b, n_pages - (j + 1) * pp + i]) + tail

    in_specs = [
        pl.BlockSpec((t, w), lambda b, j, pt: (b, qcol)),
        pl.BlockSpec((None, HD, w), lambda b, j, pt: (b, 0, 0)),
        pl.BlockSpec((None, HD, w), lambda b, j, pt: (b, 0, 0)),
        pl.BlockSpec((None, HD, nh), lambda b, j, pt: (b, 0, 0)),
    ]
    in_specs += [pl.BlockSpec((None, None, HD * nh, HD), pg(i, (0, 0))) for i in range(pp)]
    in_specs += [pl.BlockSpec((None, None, HD * nh, HD), pg(i, (0, 0))) for i in range(pp)]
    in_specs += [pl.BlockSpec((None, None, HD, nh), pg(i, (0, 0))) for i in range(pp)]
    return pl.pallas_call(
        functools.partial(_fox_sample_kernel, pp=pp, t=t, nh=nh, scale=HD ** -0.5),
        grid_spec=pltpu.PrefetchScalarGridSpec(
            num_scalar_prefetch=1,
            grid=(n, nj),
            in_specs=in_specs,
            out_specs=pl.BlockSpec((t, w), lambda b, j, pt: (b, 0)),
            scratch_shapes=[
                pltpu.VMEM((HD, w), BF16),
                pltpu.VMEM((1, HD), F32),
                pltpu.VMEM((1, HD), F32),
                pltpu.VMEM((1, HD), F32),
                pltpu.VMEM((1, HD), F32),
                pltpu.VMEM((nh * t, w), F32),
            ],
        ),
        out_shape=jax.ShapeDtypeStruct((n * t, w), F32),
        compiler_params=_params("parallel", "arbitrary"),
        name="fox_sample",
    )(page_table, zs, kn_pad, vn_pad, lfn_pad, *([cache_k] * pp), *([cache_v] * pp), *([cache_lf] * pp))


HG_CHUNK = 128
HG_MATMUL_LEVELS = 3


def _hgrn_sum_matrix(seq_rows):
    c = HG_CHUNK
    t = np.arange(c)[:, None]
    i = np.arange(c)[None, :]
    same = (t // seq_rows) == (i // seq_rows)
    blocks = [same & (i <= t), same & (i > t)]
    for lv in range(1, HG_MATMUL_LEVELS + 1):
        s = 1 << lv
        half = s // 2
        start = (t // s) * s
        upper = (t % s) >= half
        blocks.append(np.where(upper, (i >= start + half) & (i <= t), (i > t) & (i <= start + half - 1)))
    d = np.concatenate(blocks, axis=0).astype(np.float32)
    return np.concatenate([d, d, d], axis=1)


def _hgrn_lower_bound(lbl_ref, layer):
    lg = lbl_ref[...]
    e = jnp.exp(lg - jnp.max(lg, axis=0, keepdims=True))
    sm = e / jnp.sum(e, axis=0, keepdims=True)
    return jnp.sum(sm[1:layer + 1, :], axis=0, keepdims=True)


def _hgrn_gates(cf, lb):
    e = jnp.exp(-jnp.abs(cf))
    inv = 1.0 / (1.0 + e)
    pos = cf >= 0.0
    sig = jnp.where(pos, inv, e * inv)
    sig_neg = jnp.where(pos, e * inv, inv)
    return jnp.log(lb + (1.0 - lb) * sig), (1.0 - lb) * sig_neg


def _hgrn_attention(q, kk, g, dm, levels):
    c = HG_CHUNK
    sums = jnp.dot(dm, jnp.concatenate(_split3(g), axis=0), preferred_element_type=F32)
    b = sums[0:c]
    suffix = sums[c:2 * c]
    row = lax.broadcasted_iota(jnp.int32, (c, HD), 0)
    rr = lax.broadcasted_iota(jnp.int32, (c, c), 0)
    cc = lax.broadcasted_iota(jnp.int32, (c, c), 1)
    att = jnp.where(rr == cc, _dot_nt(q.astype(BF16), kk.astype(BF16)), 0.0)
    for lv in range(1, levels + 1):
        s = 1 << lv
        half = s // 2
        upper = (row & (s - 1)) >= half
        if lv <= HG_MATMUL_LEVELS:
            d = sums[(1 + lv) * c:(2 + lv) * c]
        else:
            mids = [jnp.broadcast_to(b[k * s + half - 1:k * s + half, :], (s, HD)) for k in range(c // s)]
            bmid = mids[0] if len(mids) == 1 else jnp.concatenate(mids, axis=0)
            d = jnp.where(upper, b - bmid, bmid - b)
        e = jnp.exp(d)
        qt = (q * jnp.where(upper, e, 0.0)).astype(BF16)
        kt = (kk * jnp.where(upper, 0.0, e)).astype(BF16)
        att = att + jnp.where((rr >> lv) == (cc >> lv), _dot_nt(qt, kt), 0.0)
    return att, b, suffix


def _hgrn_kernel(q_ref, f_ref, i_ref, g_ref, s0_ref, lbl_ref, gn_ref, dm_ref, y_ref, so_ref, st_s, *, nh, layer):
    c = HG_CHUNK

    @pl.when(pl.program_id(1) == 0)
    def _():
        st_s[...] = s0_ref[...]

    lb = _hgrn_lower_bound(lbl_ref, layer)
    rr = lax.broadcasted_iota(jnp.int32, (c, c), 0)
    cc = lax.broadcasted_iota(jnp.int32, (c, c), 1)
    eye = (rr == cc).astype(F32)
    dm = dm_ref[...]

    for h in range(nh):
        sl = slice(h * HD, (h + 1) * HD)
        q, v = q_ref[:, sl], i_ref[:, sl]
        g, kk = _hgrn_gates(f_ref[:, sl], lb[:, sl])
        att, b, suffix = _hgrn_attention(q, kk, g, dm, c.bit_length() - 1)
        state = st_s[h]
        vb = v.astype(BF16)
        o = jnp.dot((q * jnp.exp(b)).astype(BF16), state.astype(BF16), preferred_element_type=F32)
        o = o + jnp.dot(att.astype(BF16), vb, preferred_element_type=F32)
        kd = kk * jnp.exp(suffix)
        ecol = _col_from_row(jnp.exp(b[c - 1:c, :]), eye)
        st_s[h] = ecol * state + jnp.dot(kd.T.astype(BF16), vb, preferred_element_type=F32)
        gate = g_ref[:, sl]
        y_ref[:, sl] = _rms(o, gn_ref[...]) * (gate * jax.nn.sigmoid(gate))

    so_ref[...] = st_s[...]


def _hgrn_sample_kernel(q_ref, f_ref, i_ref, g_ref, s0_ref, lbl_ref, gn_ref, dm_ref, y_ref, so_ref,
                        qe_s, kdt_s, e_s, o_s, *, t, nh, layer):
    c = HG_CHUNK
    lb = _hgrn_lower_bound(lbl_ref, layer)
    rr = lax.broadcasted_iota(jnp.int32, (c, c), 0)
    cc = lax.broadcasted_iota(jnp.int32, (c, c), 1)
    eye = (rr == cc).astype(F32)
    dm = dm_ref[...]

    for h in range(nh):
        sl = slice(h * HD, (h + 1) * HD)
        q, v = q_ref[:, sl], i_ref[:, sl]
        g, kk = _hgrn_gates(f_ref[:, sl], lb[:, sl])
        att, b, suffix = _hgrn_attention(q, kk, g, dm, t.bit_length() - 1)
        vb = v.astype(BF16)
        e = jnp.exp(b)
        e_s[...] = e
        qe_s[...] = q * e
        kdt_s[...] = (kk * jnp.exp(suffix)).T
        o_s[...] = jnp.dot(att.astype(BF16), vb, preferred_element_type=F32)

        def sequence(n, carry):
            r = pl.multiple_of(n * t, t)
            state = s0_ref[n, h]
            o_s[pl.ds(r, t), :] += jnp.dot(qe_s[pl.ds(r, t), :].astype(BF16), state.astype(BF16),
                                           preferred_element_type=F32)
            ecol = _col_from_row(e_s[pl.ds(r, t), :][t - 1:t, :], eye)
            own = (cc >= r) & (cc < r + t)
            kdt = jnp.where(own, kdt_s[...], 0.0).astype(BF16)
            so_ref[n, h] = ecol * state + jnp.dot(kdt, vb, preferred_element_type=F32)
            return carry

        lax.fori_loop(0, c // t, sequence, 0)
        gate = g_ref[:, sl]
        y_ref[:, sl] = _rms(o_s[...], gn_ref[...]) * (gate * jax.nn.sigmoid(gate))


def hgrn_mixer(z, s0, lb_logits, g_norm, *, n, t, layer):
    nh = s0.shape[1]
    w = nh * HD
    c = HG_CHUNK
    assert t % c == 0
    nc = t // c
    dm = jnp.asarray(_hgrn_sum_matrix(c), BF16)
    col = lambda k: pl.BlockSpec((c, w), lambda i, b: (i * nc + b, k))
    return pl.pallas_call(
        functools.partial(_hgrn_kernel, nh=nh, layer=layer),
        grid=(n, nc),
        in_specs=[
            col(0), col(1), col(2), col(3),
            pl.BlockSpec((None, nh, HD, HD), lambda i, b: (i, 0, 0, 0)),
            pl.BlockSpec(lb_logits.shape, lambda i, b: (0, 0)),
            pl.BlockSpec((1, HD), lambda i, b: (0, 0)),
            pl.BlockSpec(dm.shape, lambda i, b: (0, 0)),
        ],
        out_specs=[
            col(0),
            pl.BlockSpec((None, nh, HD, HD), lambda i, b: (i, 0, 0, 0)),
        ],
        out_shape=[
            jax.ShapeDtypeStruct((n * t, w), F32),
            jax.ShapeDtypeStruct((n, nh, HD, HD), F32),
        ],
        scratch_shapes=[pltpu.VMEM((nh, HD, HD), F32)],
        compiler_params=_params("parallel", "arbitrary"),
        name="hgrn_mixer",
    )(z, z, z, z, s0, lb_logits, g_norm.reshape(1, HD), dm)


def hgrn_sample(z, s0, lb_logits, g_norm, *, s_layer, n, t, layer):
    nh = s0.shape[2]
    w = nh * HD
    c = HG_CHUNK
    assert c % t == 0 and t % 8 == 0 and t & (t - 1) == 0 and (n * t) % c == 0
    nseq = c // t
    dm = jnp.asarray(_hgrn_sum_matrix(t), BF16)
    col = lambda k: pl.BlockSpec((c, w), lambda i: (i, k))
    return pl.pallas_call(
        functools.partial(_hgrn_sample_kernel, t=t, nh=nh, layer=layer),
        grid=(n // nseq,),
        in_specs=[
            col(0), col(1), col(2), col(3),
            pl.BlockSpec((None, nseq, nh, HD, HD), lambda i: (s_layer, i, 0, 0, 0)),
            pl.BlockSpec(lb_logits.shape, lambda i: (0, 0)),
            pl.BlockSpec((1, HD), lambda i: (0, 0)),
            pl.BlockSpec(dm.shape, lambda i: (0, 0)),
        ],
        out_specs=[
            col(0),
            pl.BlockSpec((nseq, nh, HD, HD), lambda i: (i, 0, 0, 0)),
        ],
        out_shape=[
            jax.ShapeDtypeStruct((n * t, w), F32),
            jax.ShapeDtypeStruct((n, nh, HD, HD), F32),
        ],
        scratch_shapes=[pltpu.VMEM((c, HD), F32)] * 4,
        compiler_params=_params("parallel"),
        name="hgrn_sample",
    )(z, z, z, z, s0, lb_logits, g_norm.reshape(1, HD), dm)


def _merge_groups(outs, lses):
    mx = functools.reduce(jnp.maximum, lses)
    ws = [jnp.exp(l - mx) for l in lses]
    den = functools.reduce(lambda a, b: a + b, ws)
    return functools.reduce(lambda a, b: a + b, [w * o for w, o in zip(ws, outs)]) / den


DSW_BLOCK = 128


def _dsw_prompt_kernel(q0_ref, q1_ref, q2_ref, k_ref, v_ref, o_ref, og_s, lse_s, *, s_len, scale):
    qrefs = (q0_ref, q1_ref, q2_ref)
    qb = DSW_BLOCK
    i = lax.broadcasted_iota(jnp.int32, (qb, 2 * qb), 0)
    j = lax.broadcasted_iota(jnp.int32, (qb, 2 * qb), 1)
    for g, (win, dil) in enumerate(D_GROUPS):
        band = win // dil
        assert band <= qb and (s_len // dil) % qb == 0
        band_mask = (j >= i + qb - band) & (j <= i + qb)
        first_mask = band_mask[:, qb:]
        for r in range(dil):
            for lb in range(s_len // dil // qb):
                rows = pl.ds(r + lb * qb * dil, qb, stride=dil)
                q = qrefs[g][rows, :].astype(BF16)
                if lb == 0:
                    keys, mask = rows, first_mask
                else:
                    keys, mask = pl.ds(r + (lb - 1) * qb * dil, 2 * qb, stride=dil), band_mask
                s = jnp.where(mask, _dot_nt(q, k_ref[keys, :].astype(BF16)) * scale, -jnp.inf)
                m = jnp.max(s, axis=-1, keepdims=True)
                p = jnp.exp(s - m)
                l = jnp.sum(p, axis=-1, keepdims=True)
                o = jnp.dot(p.astype(BF16), v_ref[keys, :].astype(BF16), preferred_element_type=F32)
                og_s[g, rows, :] = o / l
                lse_s[g, rows, :] = jnp.broadcast_to(m + jnp.log(l), (qb, HD))
    ng = len(D_GROUPS)
    o_ref[...] = _merge_groups([og_s[g] for g in range(ng)], [lse_s[g] for g in range(ng)])


def dsw_prompt(z, *, n, s_len, nkv, qcol):
    ng = len(D_GROUPS)
    c0 = qcol * nkv
    spec = lambda k: pl.BlockSpec((s_len, HD), lambda b, h: (b, c0 + k * nkv + h))
    return pl.pallas_call(
        functools.partial(_dsw_prompt_kernel, s_len=s_len, scale=HD ** -0.5),
        grid=(n, nkv),
        in_specs=[spec(0), spec(1), spec(2), spec(ng), spec(ng + 1)],
        out_specs=pl.BlockSpec((s_len, HD), lambda b, h: (b, h)),
        out_shape=jax.ShapeDtypeStruct((n * s_len, nkv * HD), F32),
        scratch_shapes=[pltpu.VMEM((ng, s_len, HD), F32), pltpu.VMEM((ng, s_len, HD), F32)],
        compiler_params=_params("parallel", "parallel"),
        name="dsw_prompt",
    )(z, z, z, z, z)


def _dsw_sample_kernel(q0_ref, q1_ref, q2_ref, kn_ref, vn_ref, kc_ref, vc_ref, o_ref, kn_s, vn_s,
                       *, t, nkv, wbuf, scale):
    ng = len(D_GROUPS)
    kn_s[...] = jnp.zeros_like(kn_s)
    vn_s[...] = jnp.zeros_like(vn_s)
    kn_s[0:t, :] = kn_ref[...]
    vn_s[0:t, :] = vn_ref[...]

    def group_consts(shape):
        r = lax.broadcasted_iota(jnp.int32, shape, 0)
        grp = r // t
        win = jnp.zeros(shape, jnp.int32)
        dmask = jnp.zeros(shape, jnp.int32)
        for g, (w_, d_) in enumerate(D_GROUPS):
            win = jnp.where(grp == g, w_, win)
            dmask = jnp.where(grp == g, d_ - 1, dmask)
        return r % t, win, dmask

    tok_c, win_c, dm_c = group_consts((ng * t, wbuf))
    dist_c = wbuf + tok_c - lax.broadcasted_iota(jnp.int32, (ng * t, wbuf), 1)
    valid_c = (dist_c >= 0) & (dist_c <= win_c) & ((dist_c & dm_c) == 0)
    tok_n, win_n, dm_n = group_consts((ng * t, HD))
    dist_n = tok_n - lax.broadcasted_iota(jnp.int32, (ng * t, HD), 1)
    valid_n = (dist_n >= 0) & (dist_n <= win_n) & ((dist_n & dm_n) == 0)

    for h in range(nkv):
        sl = slice(h * HD, (h + 1) * HD)
        q = jnp.concatenate([q0_ref[:, sl], q1_ref[:, sl], q2_ref[:, sl]], axis=0).astype(BF16)
        kc = kc_ref[pl.ds(h, wbuf, stride=nkv), :].astype(BF16)
        vc = vc_ref[pl.ds(h, wbuf, stride=nkv), :].astype(BF16)
        s_c = jnp.where(valid_c, _dot_nt(q, kc) * scale, -jnp.inf)
        s_n = jnp.where(valid_n, _dot_nt(q, kn_s[:, sl].astype(BF16)) * scale, -jnp.inf)
        m = jnp.maximum(jnp.max(s_c, axis=-1, keepdims=True), jnp.max(s_n, axis=-1, keepdims=True))
        p_c = jnp.exp(s_c - m)
        p_n = jnp.exp(s_n - m)
        l = jnp.sum(p_c, axis=-1, keepdims=True) + jnp.sum(p_n, axis=-1, keepdims=True)
        o = jnp.dot(p_c.astype(BF16), vc, preferred_element_type=F32)
        o = (o + jnp.dot(p_n.astype(BF16), vn_s[:, sl].astype(BF16), preferred_element_type=F32)) / l
        lse = m + jnp.log(l)
        o_ref[:, sl] = _merge_groups([o[g * t:(g + 1) * t] for g in range(ng)],
                                     [lse[g * t:(g + 1) * t] for g in range(ng)])


def dsw_sample(z, cache_k, cache_v, *, layer, n, t, nkv, qcol, row0):
    w = nkv * HD
    n_layers, _, wbuf = cache_k.shape[:3]
    cache_k = cache_k.reshape(n_layers, n, wbuf * nkv, HD)
    cache_v = cache_v.reshape(n_layers, n, wbuf * nkv, HD)
    ng = len(D_GROUPS)
    rb0 = row0 // t
    col = lambda k: pl.BlockSpec((t, w), lambda i: (rb0 + i, qcol + k))
    cache = pl.BlockSpec((None, None, wbuf * nkv, HD), lambda i: (layer, i, 0, 0))
    return pl.pallas_call(
        functools.partial(_dsw_sample_kernel, t=t, nkv=nkv, wbuf=wbuf, scale=HD ** -0.5),
        grid=(n,),
        in_specs=[col(0), col(1), col(2), col(ng), col(ng + 1), cache, cache],
        out_specs=pl.BlockSpec((t, w), lambda i: (i, 0)),
        out_shape=jax.ShapeDtypeStruct((n * t, w), F32),
        scratch_shapes=[pltpu.VMEM((HD, w), F32), pltpu.VMEM((HD, w), F32)],
        compiler_params=_params("parallel"),
        name="dsw_sample",
    )(z, z, z, z, z, cache_k, cache_v)


def kernel(x_prompt, x_sample, cache_fox_k, cache_fox_v, cache_fox_logf, page_table, state_lru_conv, state_lru_h, state_hgrn, cache_dsw_k, cache_dsw_v, norm_gains, ffn_w_gate, ffn_w_up, ffn_w_down, even_w_in, even_w_out, lru_conv_w, lru_conv_b, lru_w_r, lru_b_r, lru_w_i, lru_b_i, lru_lambda, fox_b_f, odd_w_in, odd_w_out, hgrn_lb_logits, hgrn_norm, final_norm):
    n_p, s_len, d = x_prompt.shape
    n_s, t_s, _ = x_sample.shape
    depth = norm_gains.shape[0]
    a_w = lru_lambda.shape[-1]
    b_heads = fox_b_f.shape[-1]
    b_w = b_heads * HD
    c_heads = state_hgrn.shape[2]
    c_w = c_heads * HD
    d_heads = cache_dsw_k.shape[3]
    d_w = d_heads * HD
    n_pool = cache_fox_k.shape[1]
    wbuf = cache_dsw_k.shape[2]
    keep = min(max(w for w, _ in D_GROUPS), s_len)

    xp = x_prompt.reshape(n_p * s_len, d)
    xs = x_sample.reshape(n_s * t_s, d)
    tm_p, tm_s = 1024, n_s * t_s
    names = ('fox_k', 'fox_v', 'fox_logf', 'lru_conv', 'lru_h', 'hgrn', 'dsw_k', 'dsw_v')
    st_p = {nm: [] for nm in names}
    st_s = {nm: [] for nm in names}

    def ffn_both(xp, xs, l, k, final):
        g = norm_gains[l, 2 * k]
        w3 = (ffn_w_gate, ffn_w_up, ffn_w_down)
        return (ffn(xp, g, *w3, final_norm, l=l, k=k, final=final, tm=tm_p, tf=256),
                ffn(xs, g, *w3, final_norm, l=l, k=k, final=final, tm=tm_s, tf=512))

    for l in range(depth):
        j = l // 2
        xp, xs = ffn_both(xp, xs, l, 0, False)
        g_mix = norm_gains[l, 1]
        if l % 2 == 0:
            n_main = 2 * a_w + 3 * b_w
            w_in = even_w_in[j]
            w_f = jnp.pad(w_in[:, n_main:], ((0, 0), (0, HD - b_heads)))
            bfp = jnp.pad(fox_b_f[j], (0, HD - b_heads)).reshape(1, HD)
            wri = jnp.concatenate([lru_w_r[j], lru_w_i[j]], axis=-1).astype(BF16)
            w_out = even_w_out[j]
            lru_args = (lru_conv_w[j], lru_conv_b[j], wri, lru_b_r[j], lru_b_i[j], lru_lambda[j])
            qcol = 2 * a_w // b_w

            zp = norm_matmul(xp, g_mix, w_in, tm=tm_p, tn=1024)
            zfp = norm_matmul(xp, g_mix, w_f, tm=tm_p, tn=HD)
            zs = norm_matmul(xs, g_mix, w_in, tm=tm_s, tn=1024)
            zfs = norm_matmul(xs, g_mix, w_f, tm=tm_s, tn=HD)

            ya_p, cv_p, h_p = lru_mixer(zp, jnp.zeros((n_p, CONV_W - 1, a_w), F32), jnp.zeros((n_p, a_w), F32),
                                        *lru_args, n=n_p, t=s_len, tt=256, starts=True)
            lf_p, cum_p, cumt_p = fox_prep(zfp, bfp, n=n_p, s_len=s_len, nh=b_heads)
            ob_p = fox_prompt(zp, cum_p, cumt_p, n=n_p, s_len=s_len, tq=512, qcol=qcol)

            ya_s, cv_s, h_s = lru_mixer(zs, state_lru_conv[j], state_lru_h[j],
                                        *lru_args, n=n_s, t=t_s, tt=t_s, starts=False)
            lf_s, _, _ = fox_prep(zfs, bfp, n=1, s_len=n_s * t_s, nh=b_heads)
            k_s = zs[:, 2 * a_w + b_w:2 * a_w + 2 * b_w]
            v_s = zs[:, 2 * a_w + 2 * b_w:2 * a_w + 3 * b_w]
            pad_rows = lambda a: jnp.pad(a.reshape(n_s, t_s, -1), ((0, 0), (0, HD - t_s), (0, 0)))
            ob_s = fox_sample(page_table, zs, pad_rows(k_s), pad_rows(v_s), pad_rows(lf_s),
                              cache_fox_k, cache_fox_v, cache_fox_logf,
                              layer=j, n=n_s, t=t_s, nh=b_heads, qcol=qcol, pp=8)

            xp = out_proj(xp, ya_p, ob_p, w_out, tm=tm_p, tn=512)
            xs = out_proj(xs, ya_s, ob_s, w_out, tm=tm_s, tn=512)

            k_p = zp[:, 2 * a_w + b_w:2 * a_w + 2 * b_w]
            v_p = zp[:, 2 * a_w + 2 * b_w:2 * a_w + 3 * b_w]
            st_p['fox_k'].append(k_p.reshape(n_p, s_len, b_heads, HD))
            st_p['fox_v'].append(v_p.reshape(n_p, s_len, b_heads, HD))
            st_p['fox_logf'].append(lf_p.reshape(n_p, s_len, b_heads))
            st_p['lru_conv'].append(cv_p)
            st_p['lru_h'].append(h_p.reshape(n_p, a_w))
            st_s['fox_k'].append(k_s.reshape(n_s, t_s, b_heads, HD))
            st_s['fox_v'].append(v_s.reshape(n_s, t_s, b_heads, HD))
            st_s['fox_logf'].append(lf_s.reshape(n_s, t_s, b_heads))
            st_s['lru_conv'].append(cv_s)
            st_s['lru_h'].append(h_s.reshape(n_s, a_w))
        else:
            w_in = odd_w_in[j]
            w_out = odd_w_out[j]
            qcol = 4 * c_w // d_w
            zp = norm_matmul(xp, g_mix, w_in, tm=tm_p, tn=512)
            zs = norm_matmul(xs, g_mix, w_in, tm=tm_s, tn=512)

            oc_p, s_p = hgrn_mixer(zp, jnp.zeros((n_p, c_heads, HD, HD), F32), hgrn_lb_logits, hgrn_norm[j],
                                   n=n_p, t=s_len, layer=l)
            od_p = dsw_prompt(zp, n=n_p, s_len=s_len, nkv=d_heads, qcol=qcol)
            oc_s, s_s = hgrn_sample(zs, state_hgrn, hgrn_lb_logits, hgrn_norm[j],
                                    s_layer=j, n=n_s, t=t_s, layer=l)
            od_s = dsw_sample(zs, cache_dsw_k, cache_dsw_v,
                              layer=j, n=n_s, t=t_s, nkv=d_heads, qcol=qcol, row0=0)

            xp = out_proj(xp, oc_p, od_p, w_out, tm=tm_p, tn=512)
            xs = out_proj(xs, oc_s, od_s, w_out, tm=tm_s, tn=512)

            k0 = 4 * c_w + len(D_GROUPS) * d_w
            zp3 = zp.reshape(n_p, s_len, -1)
            st_p['hgrn'].append(s_p)
            st_p['dsw_k'].append(zp3[:, s_len - keep:, k0:k0 + d_w].reshape(n_p, keep, d_heads, HD))
            st_p['dsw_v'].append(zp3[:, s_len - keep:, k0 + d_w:k0 + 2 * d_w].reshape(n_p, keep, d_heads, HD))
            st_s['hgrn'].append(s_s)
            st_s['dsw_k'].append(zs[:, k0:k0 + d_w].reshape(n_s, t_s, d_heads, HD))
            st_s['dsw_v'].append(zs[:, k0 + d_w:k0 + 2 * d_w].reshape(n_s, t_s, d_heads, HD))
        xp, xs = ffn_both(xp, xs, l, 1, l == depth - 1)

    y_prompt = xp.reshape(n_p, s_len, d)
    y_sample = xs.reshape(n_s, t_s, d)
    return (y_prompt, y_sample,
            *(jnp.stack(st_p[nm]) for nm in names),
            *(jnp.stack(st_s[nm]) for nm in names))
```

```python
import functools

import numpy as np
import jax
import jax.numpy as jnp
from jax import lax
from jax.experimental import pallas as pl
from jax.experimental.pallas import tpu as pltpu

F32 = jnp.float32
BF16 = jnp.bfloat16
EPS = 1e-6
HD = 128
LRU_C = 8.0
CONV_W = 4
D_GROUPS = ((128, 1), (512, 4), (2048, 16))
VMEM_LIMIT_BYTES = 58 * 1024 * 1024
HIGHEST = lax.Precision.HIGHEST
NT_DIMS = (((1,), (1,)), ((), ()))


def _params(*sem):
    return pltpu.CompilerParams(dimension_semantics=sem, vmem_limit_bytes=VMEM_LIMIT_BYTES)


def _rms(x, g):
    ms = jnp.mean(x * x, axis=-1, keepdims=True)
    return x * lax.rsqrt(ms + EPS) * g


def _softplus(z):
    return jnp.maximum(z, 0.0) + jnp.log1p(jnp.exp(-jnp.abs(z)))


def _log_sigmoid(z):
    return -_softplus(-z)


def _dot_nt(a, b):
    return lax.dot_general(a, b, NT_DIMS, preferred_element_type=F32)


def _col_from_row(row, eye):
    return jnp.sum(eye * row, axis=1, keepdims=True)


def _ffn_kernel(x_ref, g_ref, wg_ref, wu_ref, wd_ref, gf_ref, o_ref, xn_ref, *, nf, final):
    f = pl.program_id(1)

    @pl.when(f == 0)
    def _():
        xn_ref[...] = _rms(x_ref[...], g_ref[...]).astype(BF16)
        o_ref[...] = jnp.zeros_like(o_ref)

    xn = xn_ref[...]
    a = jnp.dot(xn, wg_ref[...].astype(BF16), preferred_element_type=F32)
    b = jnp.dot(xn, wu_ref[...].astype(BF16), preferred_element_type=F32)
    h = (a * jax.nn.sigmoid(a) * b).astype(BF16)
    o_ref[...] += jnp.dot(h, wd_ref[...].astype(BF16), preferred_element_type=F32)

    @pl.when(f == nf - 1)
    def _():
        y = x_ref[...] + 0.5 * o_ref[...]
        if final:
            y = _rms(y, gf_ref[...])
        o_ref[...] = y


def ffn(x, g, wg, wu, wd, gf, *, l, k, final, tm, tf):
    m, d = x.shape
    nf = wg.shape[-1] // tf
    return pl.pallas_call(
        functools.partial(_ffn_kernel, nf=nf, final=final),
        grid=(m // tm, nf),
        in_specs=[
            pl.BlockSpec((tm, d), lambda i, f: (i, 0)),
            pl.BlockSpec((1, d), lambda i, f: (0, 0)),
            pl.BlockSpec((None, None, d, tf), lambda i, f: (l, k, 0, f)),
            pl.BlockSpec((None, None, d, tf), lambda i, f: (l, k, 0, f)),
            pl.BlockSpec((None, None, tf, d), lambda i, f: (l, k, f, 0)),
            pl.BlockSpec((1, d), lambda i, f: (0, 0)),
        ],
        out_specs=pl.BlockSpec((tm, d), lambda i, f: (i, 0)),
        out_shape=jax.ShapeDtypeStruct((m, d), F32),
        scratch_shapes=[pltpu.VMEM((tm, d), BF16)],
        compiler_params=_params("parallel", "arbitrary"),
        name="ffn",
    )(x, g.reshape(1, d), wg, wu, wd, gf.reshape(1, d))


def _norm_matmul_kernel(x_ref, g_ref, w_ref, o_ref, xn_ref):
    @pl.when(pl.program_id(1) == 0)
    def _():
        xn_ref[...] = _rms(x_ref[...], g_ref[...]).astype(BF16)

    o_ref[...] = jnp.dot(xn_ref[...], w_ref[...].astype(BF16), preferred_element_type=F32)


def norm_matmul(x, g, w, *, tm, tn):
    m, d = x.shape
    n = w.shape[1] // tn * tn
    return pl.pallas_call(
        _norm_matmul_kernel,
        grid=(m // tm, n // tn),
        in_specs=[
            pl.BlockSpec((tm, d), lambda i, j: (i, 0)),
            pl.BlockSpec((1, d), lambda i, j: (0, 0)),
            pl.BlockSpec((d, tn), lambda i, j: (0, j)),
        ],
        out_specs=pl.BlockSpec((tm, tn), lambda i, j: (i, j)),
        out_shape=jax.ShapeDtypeStruct((m, n), F32),
        scratch_shapes=[pltpu.VMEM((tm, d), BF16)],
        compiler_params=_params("parallel", "arbitrary"),
        name="norm_matmul",
    )(x, g.reshape(1, d), w)


def _out_proj_kernel(r_ref, ya_ref, yb_ref, wa_ref, wb_ref, o_ref, ya_s, yb_s):
    @pl.when(pl.program_id(1) == 0)
    def _():
        ya_s[...] = ya_ref[...].astype(BF16)
        yb_s[...] = yb_ref[...].astype(BF16)

    acc = jnp.dot(ya_s[...], wa_ref[...].astype(BF16), preferred_element_type=F32)
    acc = acc + jnp.dot(yb_s[...], wb_ref[...].astype(BF16), preferred_element_type=F32)
    o_ref[...] = r_ref[...] + acc


def out_proj(res, ya, yb, w, *, tm, tn):
    m, d = res.shape
    ka, kb = ya.shape[1], yb.shape[1]
    assert ka % kb == 0
    return pl.pallas_call(
        _out_proj_kernel,
        grid=(m // tm, d // tn),
        in_specs=[
            pl.BlockSpec((tm, tn), lambda i, j: (i, j)),
            pl.BlockSpec((tm, ka), lambda i, j: (i, 0)),
            pl.BlockSpec((tm, kb), lambda i, j: (i, 0)),
            pl.BlockSpec((ka, tn), lambda i, j: (0, j)),
            pl.BlockSpec((kb, tn), lambda i, j: (ka // kb, j)),
        ],
        out_specs=pl.BlockSpec((tm, tn), lambda i, j: (i, j)),
        out_shape=jax.ShapeDtypeStruct((m, d), F32),
        scratch_shapes=[pltpu.VMEM((tm, ka), BF16), pltpu.VMEM((tm, kb), BF16)],
        compiler_params=_params("parallel", "arbitrary"),
        name="out_proj",
    )(res, ya, yb, w, w)


def _lru_kernel(ax_ref, ag_ref, cb0_ref, h0_ref, cw_ref, cb_ref, wri_ref, br_ref, bi_ref, lam_ref,
                ya_ref, cst_ref, hl_ref, xbuf, hcar, a_s, b_s, *, tt, nh, starts):
    tb = pl.program_id(1)

    @pl.when(tb == 0)
    def _():
        xbuf[5:8, :] = cb0_ref[...]
        hcar[...] = h0_ref[...]

    x = ax_ref[...]
    xbuf[8:8 + tt, :] = x
    cw = cw_ref[...]
    y = cb_ref[...] + xbuf[5:5 + tt, :] * cw[0:1, :]
    y = y + xbuf[6:6 + tt, :] * cw[1:2, :]
    y = y + xbuf[7:7 + tt, :] * cw[2:3, :]
    y = y + x * cw[3:4, :]
    tail = xbuf[5 + tt:8 + tt, :]
    xbuf[5:8, :] = tail
    cst_ref[...] = tail

    sp = _softplus(-lam_ref[...])
    for h in range(nh):
        sl = slice(h * HD, (h + 1) * HD)
        yh = y[:, sl]
        gts = jnp.dot(yh.astype(BF16), wri_ref[h], preferred_element_type=F32)
        gr = jax.nn.sigmoid(gts[:, :HD] + br_ref[:, sl])
        gi = jax.nn.sigmoid(gts[:, HD:] + bi_ref[:, sl])
        log_a = -LRU_C * gr * sp[:, sl]
        a = jnp.exp(log_a)
        mult = jnp.sqrt(-jnp.tanh(log_a) * (a * a + 1.0))
        if starts:
            row = lax.broadcasted_iota(jnp.int32, mult.shape, 0)
            mult = jnp.where((row == 0) & (tb == 0), 1.0, mult)
        a_s[:, sl] = a
        b_s[:, sl] = mult * gi * yh

    def group(gidx, hprev):
        r = pl.multiple_of(gidx * 8, 8)
        a8 = a_s[pl.ds(r, 8), :]
        b8 = b_s[pl.ds(r, 8), :]
        row = lax.broadcasted_iota(jnp.int32, a8.shape, 0)
        for s in (1, 2, 4):
            ok = row >= s
            b8 = jnp.where(ok, a8 * pltpu.roll(b8, s, 0) + b8, b8)
            a8 = jnp.where(ok, a8 * pltpu.roll(a8, s, 0), a8)
        h8 = a8 * hprev + b8
        b_s[pl.ds(r, 8), :] = h8
        return h8[7:8, :]

    hlast = lax.fori_loop(0, tt // 8, group, hcar[...])
    hcar[...] = hlast
    hl_ref[...] = hlast
    ya_ref[...] = b_s[...] * jax.nn.gelu(ag_ref[...])


def lru_mixer(z, conv0, h0, cw, cb, wri, br, bi, lam, *, n, t, tt, starts):
    w = lam.shape[-1]
    nh = w // HD
    nb = t // tt
    row = lambda i, b: (i * nb + b, 0)
    vec = pl.BlockSpec((1, w), lambda i, b: (0, 0))
    return pl.pallas_call(
        functools.partial(_lru_kernel, tt=tt, nh=nh, starts=starts),
        grid=(n, nb),
        in_specs=[
            pl.BlockSpec((tt, w), row),
            pl.BlockSpec((tt, w), lambda i, b: (i * nb + b, 1)),
            pl.BlockSpec((None, CONV_W - 1, w), lambda i, b: (i, 0, 0)),
            pl.BlockSpec((None, 1, w), lambda i, b: (i, 0, 0)),
            pl.BlockSpec((CONV_W, w), lambda i, b: (0, 0)),
            vec,
            pl.BlockSpec((nh, HD, 2 * HD), lambda i, b: (0, 0, 0)),
            vec, vec, vec,
        ],
        out_specs=[
            pl.BlockSpec((tt, w), row),
            pl.BlockSpec((None, CONV_W - 1, w), lambda i, b: (i, 0, 0)),
            pl.BlockSpec((None, 1, w), lambda i, b: (i, 0, 0)),
        ],
        out_shape=[
            jax.ShapeDtypeStruct((n * t, w), F32),
            jax.ShapeDtypeStruct((n, CONV_W - 1, w), F32),
            jax.ShapeDtypeStruct((n, 1, w), F32),
        ],
        scratch_shapes=[
            pltpu.VMEM((tt + 8, w), F32),
            pltpu.VMEM((1, w), F32),
            pltpu.VMEM((tt, w), F32),
            pltpu.VMEM((tt, w), F32),
        ],
        compiler_params=_params("parallel", "arbitrary"),
        name="lru_mixer",
    )(z, z, conv0, h0.reshape(n, 1, w), cw, cb.reshape(1, w), wri,
      br.reshape(1, w), bi.reshape(1, w), lam.reshape(1, w))


def _fox_prep_kernel(zf_ref, bf_ref, lf_ref, cum_ref, cumt_ref, cum_s, cumt_s, *, s_len, nh):
    lf = _log_sigmoid(zf_ref[...] + bf_ref[...])
    lf_ref[...] = lf[:, :nh]
    cum_s[...] = lf
    r = lax.broadcasted_iota(jnp.int32, (HD, HD), 0)
    c = lax.broadcasted_iota(jnp.int32, (HD, HD), 1)
    tri = (r >= c).astype(F32)

    def chunk(i, carry):
        st = pl.multiple_of(i * HD, HD)
        cs = jnp.dot(tri, cum_s[pl.ds(st, HD), :], preferred_element_type=F32, precision=HIGHEST) + carry
        cum_s[pl.ds(st, HD), :] = cs
        cumt_s[:, pl.ds(st, HD)] = cs.T
        return cs[HD - 1:HD, :]

    lax.fori_loop(0, s_len // HD, chunk, jnp.zeros((1, HD), F32))
    cum_ref[...] = cum_s[:, :nh]
    cumt_ref[...] = cumt_s[:nh, :]


def fox_prep(zf, bfp, *, n, s_len, nh):
    return pl.pallas_call(
        functools.partial(_fox_prep_kernel, s_len=s_len, nh=nh),
        grid=(n,),
        in_specs=[
            pl.BlockSpec((s_len, HD), lambda i: (i, 0)),
            pl.BlockSpec((1, HD), lambda i: (0, 0)),
        ],
        out_specs=[
            pl.BlockSpec((s_len, nh), lambda i: (i, 0)),
            pl.BlockSpec((s_len, nh), lambda i: (i, 0)),
            pl.BlockSpec((None, nh, s_len), lambda i: (i, 0, 0)),
        ],
        out_shape=[
            jax.ShapeDtypeStruct((n * s_len, nh), F32),
            jax.ShapeDtypeStruct((n * s_len, nh), F32),
            jax.ShapeDtypeStruct((n, nh, s_len), F32),
        ],
        scratch_shapes=[pltpu.VMEM((s_len, HD), F32), pltpu.VMEM((HD, s_len), F32)],
        compiler_params=_params("parallel"),
        name="fox_prep",
    )(zf, bfp)


LOG2E = 1.4426950408889634


def _fox_prompt_kernel(q_ref, k_ref, v_ref, cq_ref, ck_ref, o_ref, m_s, l_s, acc_s, *, tq, nh, scale):
    qi = pl.program_id(1)
    kj = pl.program_id(2)

    @pl.when(kj == 0)
    def _():
        m_s[...] = jnp.full_like(m_s, -jnp.inf)
        l_s[...] = jnp.zeros_like(l_s)
        acc_s[...] = jnp.zeros_like(acc_s)

    def block(causal):
        for h in range(nh):
            sl = slice(h * HD, (h + 1) * HD)
            u = _dot_nt(k_ref[:, sl].astype(BF16), q_ref[:, sl].astype(BF16)) * (scale * LOG2E)
            u = u - ck_ref[:, h:h + 1] * LOG2E
            if causal is not None:
                u = jnp.where(causal, u, -jnp.inf)
            cq = cq_ref[h:h + 1, :] * LOG2E
            m_prev = m_s[h]
            m_new = jnp.maximum(m_prev, jnp.max(u, axis=0, keepdims=True) + cq)
            alpha = jnp.exp2(m_prev - m_new)
            p = jnp.exp2(u - (m_new - cq))
            l_s[h] = alpha * l_s[h] + jnp.sum(p, axis=0, keepdims=True)
            acc_s[sl, :] = alpha * acc_s[sl, :] + jnp.dot(
                v_ref[:, sl].T.astype(BF16), p.astype(BF16), preferred_element_type=F32)
            m_s[h] = m_new

    @pl.when(kj < qi)
    def _():
        block(None)

    @pl.when(kj == qi)
    def _():
        key = lax.broadcasted_iota(jnp.int32, (tq, tq), 0)
        qry = lax.broadcasted_iota(jnp.int32, (tq, tq), 1)
        block(key <= qry)
        for h in range(nh):
            sl = slice(h * HD, (h + 1) * HD)
            o_ref[:, sl] = (acc_s[sl, :] / l_s[h]).T


def fox_prompt(z, cum, cumt, *, n, s_len, tq, qcol):
    nh = cum.shape[1]
    w = nh * HD
    nq = s_len // tq
    return pl.pallas_call(
        functools.partial(_fox_prompt_kernel, tq=tq, nh=nh, scale=HD ** -0.5),
        grid=(n, nq, nq),
        in_specs=[
            pl.BlockSpec((tq, w), lambda i, a, b: (i * nq + a, qcol)),
            pl.BlockSpec((tq, w), lambda i, a, b: (i * nq + jnp.minimum(a, b), qcol + 1)),
            pl.BlockSpec((tq, w), lambda i, a, b: (i * nq + jnp.minimum(a, b), qcol + 2)),
            pl.BlockSpec((None, nh, tq), lambda i, a, b: (i, 0, a)),
            pl.BlockSpec((tq, nh), lambda i, a, b: (i * nq + jnp.minimum(a, b), 0)),
        ],
        out_specs=pl.BlockSpec((tq, w), lambda i, a, b: (i * nq + a, 0)),
        out_shape=jax.ShapeDtypeStruct((n * s_len, w), F32),
        scratch_shapes=[
            pltpu.VMEM((nh, 1, tq), F32),
            pltpu.VMEM((nh, 1, tq), F32),
            pltpu.VMEM((w, tq), F32),
        ],
        compiler_params=_params("parallel", "parallel", "arbitrary"),
        name="fox_prompt",
    )(z, z, z, cumt, cum)


def _split3(x):
    hi = x.astype(BF16)
    r1 = x - hi.astype(F32)
    mid = r1.astype(BF16)
    return hi, mid, (r1 - mid.astype(F32)).astype(BF16)


def _fox_sample_kernel(pt_ref, q_ref, kn_ref, vn_ref, lfn_ref, *refs, pp, t, nh, scale):
    kc, vc, lc = refs[0:pp], refs[pp:2 * pp], refs[2 * pp:3 * pp]
    o_ref = refs[3 * pp]
    qb_s, crow_s, carry_s, m_s, l_s, acc_s = refs[3 * pp + 1:]
    j = pl.program_id(1)
    nj = pl.num_programs(1)
    ncol = nh * t

    r128 = lax.broadcasted_iota(jnp.int32, (HD, HD), 0)
    c128 = lax.broadcasted_iota(jnp.int32, (HD, HD), 1)
    eye = (r128 == c128).astype(F32)
    later_in_page = (c128 > r128).astype(BF16)
    hrow = lax.broadcasted_iota(jnp.int32, (nh, HD), 0)
    hcol = lax.broadcasted_iota(jnp.int32, (nh, HD), 1)
    expand = ((hcol // t == hrow) & (hcol < ncol)).astype(BF16)

    def to_columns(lf):
        return [jnp.dot(part, expand, preferred_element_type=F32).astype(BF16) for part in _split3(lf)]

    def block(k2d, v2d, lf, valid):
        parts = jnp.concatenate(to_columns(lf), axis=1)
        carry = carry_s[...]
        crow = crow_s[...]
        later = []
        for i in reversed(range(k2d.shape[0] // HD)):
            page = parts[i * HD:(i + 1) * HD]
            within = jnp.dot(later_in_page, page, preferred_element_type=F32)
            later.append(within[:, :HD] + within[:, HD:2 * HD] + within[:, 2 * HD:] + (carry + crow))
            tot = jnp.sum(page.astype(F32), axis=0, keepdims=True)
            carry = carry + (tot[:, :HD] + tot[:, HD:2 * HD] + tot[:, 2 * HD:])
        carry_s[...] = carry
        later = later[0] if len(later) == 1 else jnp.concatenate(later[::-1], axis=0)
        s = _dot_nt(k2d, qb_s[...]) * scale + later
        if valid is not None:
            s = jnp.where(valid, s, -jnp.inf)
        m_prev = m_s[...]
        m_new = jnp.maximum(m_prev, jnp.max(s, axis=0, keepdims=True))
        alpha = jnp.exp(m_prev - m_new)
        p = jnp.exp(s - m_new)
        l_s[...] = alpha * l_s[...] + jnp.sum(p, axis=0, keepdims=True)
        m_s[...] = m_new
        acc_s[...] = _col_from_row(alpha, eye)[:ncol] * acc_s[...] + jnp.dot(
            p.T[:ncol].astype(BF16), v2d, preferred_element_type=F32)

    def heads_to_lanes(ref):
        return jnp.concatenate([ref[pl.ds(h, HD, stride=nh), :] for h in range(nh)], axis=1).astype(BF16)

    @pl.when(j == 0)
    def _():
        q = q_ref[...]
        w = q.shape[1]
        qrep = jnp.concatenate([q] * nh + [jnp.zeros((HD - ncol, w), F32)], axis=0)
        rr = lax.broadcasted_iota(jnp.int32, (HD, w), 0)
        cc = lax.broadcasted_iota(jnp.int32, (HD, w), 1)
        qb_s[...] = jnp.where(rr // t == cc // HD, qrep, 0.0).astype(BF16)
        m_s[...] = jnp.full_like(m_s, -jnp.inf)
        l_s[...] = jnp.zeros_like(l_s)
        acc_s[...] = jnp.zeros_like(acc_s)
        carry_s[...] = jnp.zeros_like(carry_s)
        lfn = lfn_ref[...]
        lfn_cols = functools.reduce(lambda a, b: a + b, [p.astype(F32) for p in to_columns(lfn)])
        after = r128 > (c128 % t)
        crow_s[...] = -jnp.sum(jnp.where(after, lfn_cols, 0.0), axis=0, keepdims=True)
        block(kn_ref[...].astype(BF16), vn_ref[...].astype(BF16), lfn, r128 <= (c128 % t))

    block(jnp.concatenate([heads_to_lanes(r) for r in kc], axis=0),
          jnp.concatenate([heads_to_lanes(r) for r in vc], axis=0),
          jnp.concatenate([r[...] for r in lc], axis=0), None)

    @pl.when(j == nj - 1)
    def _():
        lcol = _col_from_row(l_s[...], eye)
        for h in range(nh):
            o_ref[:, h * HD:(h + 1) * HD] = (
                acc_s[h * t:(h + 1) * t, h * HD:(h + 1) * HD] / lcol[h * t:(h + 1) * t, :])


def fox_sample(page_table, zs, kn_pad, vn_pad, lfn_pad, cache_k, cache_v, cache_lf, *, layer, n, t, nh, qcol, pp):
    w = nh * HD
    n_layers, n_pool = cache_k.shape[:2]
    cache_k = cache_k.reshape(n_layers, n_pool, HD * nh, HD)
    cache_v = cache_v.reshape(n_layers, n_pool, HD * nh, HD)
    n_pages = page_table.shape[1]
    nj = n_pages // pp

    def pg(i, tail):
        return lambda b, j, pt: (layer, pt[b, n_pages - (j + 1) * pp + i]) + tail

    in_specs = [
        pl.BlockSpec((t, w), lambda b, j, pt: (b, qcol)),
        pl.BlockSpec((None, HD, w), lambda b, j, pt: (b, 0, 0)),
        pl.BlockSpec((None, HD, w), lambda b, j, pt: (b, 0, 0)),
        pl.BlockSpec((None, HD, nh), lambda b, j, pt: (b, 0, 0)),
    ]
    in_specs += [pl.BlockSpec((None, None, HD * nh, HD), pg(i, (0, 0))) for i in range(pp)]
    in_specs += [pl.BlockSpec((None, None, HD * nh, HD), pg(i, (0, 0))) for i in range(pp)]
    in_specs += [pl.BlockSpec((None, None, HD, nh), pg(i, (0, 0))) for i in range(pp)]
    return pl.pallas_call(
        functools.partial(_fox_sample_kernel, pp=pp, t=t, nh=nh, scale=HD ** -0.5),
        grid_spec=pltpu.PrefetchScalarGridSpec(
            num_scalar_prefetch=1,
            grid=(n, nj),
            in_specs=in_specs,
            out_specs=pl.BlockSpec((t, w), lambda b, j, pt: (b, 0)),
            scratch_shapes=[
                pltpu.VMEM((HD, w), BF16),
                pltpu.VMEM((1, HD), F32),
                pltpu.VMEM((1, HD), F32),
                pltpu.VMEM((1, HD), F32),
                pltpu.VMEM((1, HD), F32),
                pltpu.VMEM((nh * t, w), F32),
            ],
        ),
        out_shape=jax.ShapeDtypeStruct((n * t, w), F32),
        compiler_params=_params("parallel", "arbitrary"),
        name="fox_sample",
    )(page_table, zs, kn_pad, vn_pad, lfn_pad, *([cache_k] * pp), *([cache_v] * pp), *([cache_lf] * pp))


HG_CHUNK = 128
HG_MATMUL_LEVELS = 3


def _hgrn_sum_matrix(seq_rows):
    c = HG_CHUNK
    t = np.arange(c)[:, None]
    i = np.arange(c)[None, :]
    same = (t // seq_rows) == (i // seq_rows)
    blocks = [same & (i <= t), same & (i > t)]
    for lv in range(1, HG_MATMUL_LEVELS + 1):
        s = 1 << lv
        half = s // 2
        start = (t // s) * s
        upper = (t % s) >= half
        blocks.append(np.where(upper, (i >= start + half) & (i <= t), (i > t) & (i <= start + half - 1)))
    d = np.concatenate(blocks, axis=0).astype(np.float32)
    return np.concatenate([d, d, d], axis=1)


def _hgrn_lower_bound(lbl_ref, layer):
    lg = lbl_ref[...]
    e = jnp.exp(lg - jnp.max(lg, axis=0, keepdims=True))
    sm = e / jnp.sum(e, axis=0, keepdims=True)
    return jnp.sum(sm[1:layer + 1, :], axis=0, keepdims=True)


def _hgrn_gates(cf, lb):
    e = jnp.exp(-jnp.abs(cf))
    inv = 1.0 / (1.0 + e)
    pos = cf >= 0.0
    sig = jnp.where(pos, inv, e * inv)
    sig_neg = jnp.where(pos, e * inv, inv)
    return jnp.log(lb + (1.0 - lb) * sig), (1.0 - lb) * sig_neg


def _hgrn_attention(q, kk, g, dm, levels):
    c = HG_CHUNK
    sums = jnp.dot(dm, jnp.concatenate(_split3(g), axis=0), preferred_element_type=F32)
    b = sums[0:c]
    suffix = sums[c:2 * c]
    row = lax.broadcasted_iota(jnp.int32, (c, HD), 0)
    rr = lax.broadcasted_iota(jnp.int32, (c, c), 0)
    cc = lax.broadcasted_iota(jnp.int32, (c, c), 1)
    att = jnp.where(rr == cc, _dot_nt(q.astype(BF16), kk.astype(BF16)), 0.0)
    for lv in range(1, levels + 1):
        s = 1 << lv
        half = s // 2
        upper = (row & (s - 1)) >= half
        if lv <= HG_MATMUL_LEVELS:
            d = sums[(1 + lv) * c:(2 + lv) * c]
        else:
            mids = [jnp.broadcast_to(b[k * s + half - 1:k * s + half, :], (s, HD)) for k in range(c // s)]
            bmid = mids[0] if len(mids) == 1 else jnp.concatenate(mids, axis=0)
            d = jnp.where(upper, b - bmid, bmid - b)
        e = jnp.exp(d)
        qt = (q * jnp.where(upper, e, 0.0)).astype(BF16)
        kt = (kk * jnp.where(upper, 0.0, e)).astype(BF16)
        att = att + jnp.where((rr >> lv) == (cc >> lv), _dot_nt(qt, kt), 0.0)
    return att, b, suffix


def _hgrn_kernel(q_ref, f_ref, i_ref, g_ref, s0_ref, lbl_ref, gn_ref, dm_ref, y_ref, so_ref, st_s, *, nh, layer):
    c = HG_CHUNK

    @pl.when(pl.program_id(1) == 0)
    def _():
        st_s[...] = s0_ref[...]

    lb = _hgrn_lower_bound(lbl_ref, layer)
    rr = lax.broadcasted_iota(jnp.int32, (c, c), 0)
    cc = lax.broadcasted_iota(jnp.int32, (c, c), 1)
    eye = (rr == cc).astype(F32)
    dm = dm_ref[...]

    for h in range(nh):
        sl = slice(h * HD, (h + 1) * HD)
        q, v = q_ref[:, sl], i_ref[:, sl]
        g, kk = _hgrn_gates(f_ref[:, sl], lb[:, sl])
        att, b, suffix = _hgrn_attention(q, kk, g, dm, c.bit_length() - 1)
        state = st_s[h]
        vb = v.astype(BF16)
        o = jnp.dot((q * jnp.exp(b)).astype(BF16), state.astype(BF16), preferred_element_type=F32)
        o = o + jnp.dot(att.astype(BF16), vb, preferred_element_type=F32)
        kd = kk * jnp.exp(suffix)
        ecol = _col_from_row(jnp.exp(b[c - 1:c, :]), eye)
        st_s[h] = ecol * state + jnp.dot(kd.T.astype(BF16), vb, preferred_element_type=F32)
        gate = g_ref[:, sl]
        y_ref[:, sl] = _rms(o, gn_ref[...]) * (gate * jax.nn.sigmoid(gate))

    so_ref[...] = st_s[...]


def _hgrn_sample_kernel(q_ref, f_ref, i_ref, g_ref, s0_ref, lbl_ref, gn_ref, dm_ref, y_ref, so_ref,
                        qe_s, kdt_s, e_s, o_s, *, t, nh, layer):
    c = HG_CHUNK
    lb = _hgrn_lower_bound(lbl_ref, layer)
    rr = lax.broadcasted_iota(jnp.int32, (c, c), 0)
    cc = lax.broadcasted_iota(jnp.int32, (c, c), 1)
    eye = (rr == cc).astype(F32)
    dm = dm_ref[...]

    for h in range(nh):
        sl = slice(h * HD, (h + 1) * HD)
        q, v = q_ref[:, sl], i_ref[:, sl]
        g, kk = _hgrn_gates(f_ref[:, sl], lb[:, sl])
        att, b, suffix = _hgrn_attention(q, kk, g, dm, t.bit_length() - 1)
        vb = v.astype(BF16)
        e = jnp.exp(b)
        e_s[...] = e
        qe_s[...] = q * e
        kdt_s[...] = (kk * jnp.exp(suffix)).T
        o_s[...] = jnp.dot(att.astype(BF16), vb, preferred_element_type=F32)

        def sequence(n, carry):
            r = pl.multiple_of(n * t, t)
            state = s0_ref[n, h]
            o_s[pl.ds(r, t), :] += jnp.dot(qe_s[pl.ds(r, t), :].astype(BF16), state.astype(BF16),
                                           preferred_element_type=F32)
            ecol = _col_from_row(e_s[pl.ds(r, t), :][t - 1:t, :], eye)
            own = (cc >= r) & (cc < r + t)
            kdt = jnp.where(own, kdt_s[...], 0.0).astype(BF16)
            so_ref[n, h] = ecol * state + jnp.dot(kdt, vb, preferred_element_type=F32)
            return carry

        lax.fori_loop(0, c // t, sequence, 0)
        gate = g_ref[:, sl]
        y_ref[:, sl] = _rms(o_s[...], gn_ref[...]) * (gate * jax.nn.sigmoid(gate))


def hgrn_mixer(z, s0, lb_logits, g_norm, *, n, t, layer):
    nh = s0.shape[1]
    w = nh * HD
    c = HG_CHUNK
    assert t % c == 0
    nc = t // c
    dm = jnp.asarray(_hgrn_sum_matrix(c), BF16)
    col = lambda k: pl.BlockSpec((c, w), lambda i, b: (i * nc + b, k))
    return pl.pallas_call(
        functools.partial(_hgrn_kernel, nh=nh, layer=layer),
        grid=(n, nc),
        in_specs=[
            col(0), col(1), col(2), col(3),
            pl.BlockSpec((None, nh, HD, HD), lambda i, b: (i, 0, 0, 0)),
            pl.BlockSpec(lb_logits.shape, lambda i, b: (0, 0)),
            pl.BlockSpec((1, HD), lambda i, b: (0, 0)),
            pl.BlockSpec(dm.shape, lambda i, b: (0, 0)),
        ],
        out_specs=[
            col(0),
            pl.BlockSpec((None, nh, HD, HD), lambda i, b: (i, 0, 0, 0)),
        ],
        out_shape=[
            jax.ShapeDtypeStruct((n * t, w), F32),
            jax.ShapeDtypeStruct((n, nh, HD, HD), F32),
        ],
        scratch_shapes=[pltpu.VMEM((nh, HD, HD), F32)],
        compiler_params=_params("parallel", "arbitrary"),
        name="hgrn_mixer",
    )(z, z, z, z, s0, lb_logits, g_norm.reshape(1, HD), dm)


def hgrn_sample(z, s0, lb_logits, g_norm, *, s_layer, n, t, layer):
    nh = s0.shape[2]
    w = nh * HD
    c = HG_CHUNK
    assert c % t == 0 and t % 8 == 0 and t & (t - 1) == 0 and (n * t) % c == 0
    nseq = c // t
    dm = jnp.asarray(_hgrn_sum_matrix(t), BF16)
    col = lambda k: pl.BlockSpec((c, w), lambda i: (i, k))
    return pl.pallas_call(
        functools.partial(_hgrn_sample_kernel, t=t, nh=nh, layer=layer),
        grid=(n // nseq,),
        in_specs=[
            col(0), col(1), col(2), col(3),
            pl.BlockSpec((None, nseq, nh, HD, HD), lambda i: (s_layer, i, 0, 0, 0)),
            pl.BlockSpec(lb_logits.shape, lambda i: (0, 0)),
            pl.BlockSpec((1, HD), lambda i: (0, 0)),
            pl.BlockSpec(dm.shape, lambda i: (0, 0)),
        ],
        out_specs=[
            col(0),
            pl.BlockSpec((nseq, nh, HD, HD), lambda i: (i, 0, 0, 0)),
        ],
        out_shape=[
            jax.ShapeDtypeStruct((n * t, w), F32),
            jax.ShapeDtypeStruct((n, nh, HD, HD), F32),
        ],
        scratch_shapes=[pltpu.VMEM((c, HD), F32)] * 4,
        compiler_params=_params("parallel"),
        name="hgrn_sample",
    )(z, z, z, z, s0, lb_logits, g_norm.reshape(1, HD), dm)


def _merge_groups(outs, lses):
    mx = functools.reduce(jnp.maximum, lses)
    ws = [jnp.exp(l - mx) for l in lses]
    den = functools.reduce(lambda a, b: a + b, ws)
    return functools.reduce(lambda a, b: a + b, [w * o for w, o in zip(ws, outs)]) / den


DSW_BLOCK = 128


def _dsw_prompt_kernel(q0_ref, q1_ref, q2_ref, k_ref, v_ref, o_ref, og_s, lse_s, *, s_len, scale):
    qrefs = (q0_ref, q1_ref, q2_ref)
    qb = DSW_BLOCK
    i = lax.broadcasted_iota(jnp.int32, (qb, 2 * qb), 0)
    j = lax.broadcasted_iota(jnp.int32, (qb, 2 * qb), 1)
    for g, (win, dil) in enumerate(D_GROUPS):
        band = win // dil
        assert band <= qb and (s_len // dil) % qb == 0
        band_mask = (j >= i + qb - band) & (j <= i + qb)
        first_mask = band_mask[:, qb:]
        for r in range(dil):
            for lb in range(s_len // dil // qb):
                rows = pl.ds(r + lb * qb * dil, qb, stride=dil)
                q = qrefs[g][rows, :].astype(BF16)
                if lb == 0:
                    keys, mask = rows, first_mask
                else:
                    keys, mask = pl.ds(r + (lb - 1) * qb * dil, 2 * qb, stride=dil), band_mask
                s = jnp.where(mask, _dot_nt(q, k_ref[keys, :].astype(BF16)) * scale, -jnp.inf)
                m = jnp.max(s, axis=-1, keepdims=True)
                p = jnp.exp(s - m)
                l = jnp.sum(p, axis=-1, keepdims=True)
                o = jnp.dot(p.astype(BF16), v_ref[keys, :].astype(BF16), preferred_element_type=F32)
                og_s[g, rows, :] = o / l
                lse_s[g, rows, :] = jnp.broadcast_to(m + jnp.log(l), (qb, HD))
    ng = len(D_GROUPS)
    o_ref[...] = _merge_groups([og_s[g] for g in range(ng)], [lse_s[g] for g in range(ng)])


def dsw_prompt(z, *, n, s_len, nkv, qcol):
    ng = len(D_GROUPS)
    c0 = qcol * nkv
    spec = lambda k: pl.BlockSpec((s_len, HD), lambda b, h: (b, c0 + k * nkv + h))
    return pl.pallas_call(
        functools.partial(_dsw_prompt_kernel, s_len=s_len, scale=HD ** -0.5),
        grid=(n, nkv),
        in_specs=[spec(0), spec(1), spec(2), spec(ng), spec(ng + 1)],
        out_specs=pl.BlockSpec((s_len, HD), lambda b, h: (b, h)),
        out_shape=jax.ShapeDtypeStruct((n * s_len, nkv * HD), F32),
        scratch_shapes=[pltpu.VMEM((ng, s_len, HD), F32), pltpu.VMEM((ng, s_len, HD), F32)],
        compiler_params=_params("parallel", "parallel"),
        name="dsw_prompt",
    )(z, z, z, z, z)


def _dsw_sample_kernel(q0_ref, q1_ref, q2_ref, kn_ref, vn_ref, kc_ref, vc_ref, o_ref, kn_s, vn_s,
                       *, t, nkv, wbuf, scale):
    ng = len(D_GROUPS)
    kn_s[...] = jnp.zeros_like(kn_s)
    vn_s[...] = jnp.zeros_like(vn_s)
    kn_s[0:t, :] = kn_ref[...]
    vn_s[0:t, :] = vn_ref[...]

    def group_consts(shape):
        r = lax.broadcasted_iota(jnp.int32, shape, 0)
        grp = r // t
        win = jnp.zeros(shape, jnp.int32)
        dmask = jnp.zeros(shape, jnp.int32)
        for g, (w_, d_) in enumerate(D_GROUPS):
            win = jnp.where(grp == g, w_, win)
            dmask = jnp.where(grp == g, d_ - 1, dmask)
        return r % t, win, dmask

    tok_c, win_c, dm_c = group_consts((ng * t, wbuf))
    dist_c = wbuf + tok_c - lax.broadcasted_iota(jnp.int32, (ng * t, wbuf), 1)
    valid_c = (dist_c >= 0) & (dist_c <= win_c) & ((dist_c & dm_c) == 0)
    tok_n, win_n, dm_n = group_consts((ng * t, HD))
    dist_n = tok_n - lax.broadcasted_iota(jnp.int32, (ng * t, HD), 1)
    valid_n = (dist_n >= 0) & (dist_n <= win_n) & ((dist_n & dm_n) == 0)

    for h in range(nkv):
        sl = slice(h * HD, (h + 1) * HD)
        q = jnp.concatenate([q0_ref[:, sl], q1_ref[:, sl], q2_ref[:, sl]], axis=0).astype(BF16)
        kc = kc_ref[pl.ds(h, wbuf, stride=nkv), :].astype(BF16)
        vc = vc_ref[pl.ds(h, wbuf, stride=nkv), :].astype(BF16)
        s_c = jnp.where(valid_c, _dot_nt(q, kc) * scale, -jnp.inf)
        s_n = jnp.where(valid_n, _dot_nt(q, kn_s[:, sl].astype(BF16)) * scale, -jnp.inf)
        m = jnp.maximum(jnp.max(s_c, axis=-1, keepdims=True), jnp.max(s_n, axis=-1, keepdims=True))
        p_c = jnp.exp(s_c - m)
        p_n = jnp.exp(s_n - m)
        l = jnp.sum(p_c, axis=-1, keepdims=True) + jnp.sum(p_n, axis=-1, keepdims=True)
        o = jnp.dot(p_c.astype(BF16), vc, preferred_element_type=F32)
        o = (o + jnp.dot(p_n.astype(BF16), vn_s[:, sl].astype(BF16), preferred_element_type=F32)) / l
        lse = m + jnp.log(l)
        o_ref[:, sl] = _merge_groups([o[g * t:(g + 1) * t] for g in range(ng)],
                                     [lse[g * t:(g + 1) * t] for g in range(ng)])


def dsw_sample(z, cache_k, cache_v, *, layer, n, t, nkv, qcol, row0):
    w = nkv * HD
    n_layers, _, wbuf = cache_k.shape[:3]
    cache_k = cache_k.reshape(n_layers, n, wbuf * nkv, HD)
    cache_v = cache_v.reshape(n_layers, n, wbuf * nkv, HD)
    ng = len(D_GROUPS)
    rb0 = row0 // t
    col = lambda k: pl.BlockSpec((t, w), lambda i: (rb0 + i, qcol + k))
    cache = pl.BlockSpec((None, None, wbuf * nkv, HD), lambda i: (layer, i, 0, 0))
    return pl.pallas_call(
        functools.partial(_dsw_sample_kernel, t=t, nkv=nkv, wbuf=wbuf, scale=HD ** -0.5),
        grid=(n,),
        in_specs=[col(0), col(1), col(2), col(ng), col(ng + 1), cache, cache],
        out_specs=pl.BlockSpec((t, w), lambda i: (i, 0)),
        out_shape=jax.ShapeDtypeStruct((n * t, w), F32),
        scratch_shapes=[pltpu.VMEM((HD, w), F32), pltpu.VMEM((HD, w), F32)],
        compiler_params=_params("parallel"),
        name="dsw_sample",
    )(z, z, z, z, z, cache_k, cache_v)


def kernel(x_prompt, x_sample, cache_fox_k, cache_fox_v, cache_fox_logf, page_table, state_lru_conv, state_lru_h, state_hgrn, cache_dsw_k, cache_dsw_v, norm_gains, ffn_w_gate, ffn_w_up, ffn_w_down, even_w_in, even_w_out, lru_conv_w, lru_conv_b, lru_w_r, lru_b_r, lru_w_i, lru_b_i, lru_lambda, fox_b_f, odd_w_in, odd_w_out, hgrn_lb_logits, hgrn_norm, final_norm):
    n_p, s_len, d = x_prompt.shape
    n_s, t_s, _ = x_sample.shape
    depth = norm_gains.shape[0]
    a_w = lru_lambda.shape[-1]
    b_heads = fox_b_f.shape[-1]
    b_w = b_heads * HD
    c_heads = state_hgrn.shape[2]
    c_w = c_heads * HD
    d_heads = cache_dsw_k.shape[3]
    d_w = d_heads * HD
    n_pool = cache_fox_k.shape[1]
    wbuf = cache_dsw_k.shape[2]
    keep = min(max(w for w, _ in D_GROUPS), s_len)

    xp = x_prompt.reshape(n_p * s_len, d)
    xs = x_sample.reshape(n_s * t_s, d)
    tm_p, tm_s = 1024, n_s * t_s
    names = ('fox_k', 'fox_v', 'fox_logf', 'lru_conv', 'lru_h', 'hgrn', 'dsw_k', 'dsw_v')
    st_p = {nm: [] for nm in names}
    st_s = {nm: [] for nm in names}

    def ffn_both(xp, xs, l, k, final):
        g = norm_gains[l, 2 * k]
        w3 = (ffn_w_gate, ffn_w_up, ffn_w_down)
        return (ffn(xp, g, *w3, final_norm, l=l, k=k, final=final, tm=tm_p, tf=256),
                ffn(xs, g, *w3, final_norm, l=l, k=k, final=final, tm=tm_s, tf=512))

    for l in range(depth):
        j = l // 2
        xp, xs = ffn_both(xp, xs, l, 0, False)
        g_mix = norm_gains[l, 1]
        if l % 2 == 0:
            n_main = 2 * a_w + 3 * b_w
            w_in = even_w_in[j]
            w_f = jnp.pad(w_in[:, n_main:], ((0, 0), (0, HD - b_heads)))
            bfp = jnp.pad(fox_b_f[j], (0, HD - b_heads)).reshape(1, HD)
            wri = jnp.concatenate([lru_w_r[j], lru_w_i[j]], axis=-1).astype(BF16)
            w_out = even_w_out[j]
            lru_args = (lru_conv_w[j], lru_conv_b[j], wri, lru_b_r[j], lru_b_i[j], lru_lambda[j])
            qcol = 2 * a_w // b_w

            zp = norm_matmul(xp, g_mix, w_in, tm=tm_p, tn=1024)
            zfp = norm_matmul(xp, g_mix, w_f, tm=tm_p, tn=HD)
            zs = norm_matmul(xs, g_mix, w_in, tm=tm_s, tn=1024)
            zfs = norm_matmul(xs, g_mix, w_f, tm=tm_s, tn=HD)

            ya_p, cv_p, h_p = lru_mixer(zp, jnp.zeros((n_p, CONV_W - 1, a_w), F32), jnp.zeros((n_p, a_w), F32),
                                        *lru_args, n=n_p, t=s_len, tt=256, starts=True)
            lf_p, cum_p, cumt_p = fox_prep(zfp, bfp, n=n_p, s_len=s_len, nh=b_heads)
            ob_p = fox_prompt(zp, cum_p, cumt_p, n=n_p, s_len=s_len, tq=512, qcol=qcol)

            ya_s, cv_s, h_s = lru_mixer(zs, state_lru_conv[j], state_lru_h[j],
                                        *lru_args, n=n_s, t=t_s, tt=t_s, starts=False)
            lf_s, _, _ = fox_prep(zfs, bfp, n=1, s_len=n_s * t_s, nh=b_heads)
            k_s = zs[:, 2 * a_w + b_w:2 * a_w + 2 * b_w]
            v_s = zs[:, 2 * a_w + 2 * b_w:2 * a_w + 3 * b_w]
            pad_rows = lambda a: jnp.pad(a.reshape(n_s, t_s, -1), ((0, 0), (0, HD - t_s), (0, 0)))
            ob_s = fox_sample(page_table, zs, pad_rows(k_s), pad_rows(v_s), pad_rows(lf_s),
                              cache_fox_k, cache_fox_v, cache_fox_logf,
                              layer=j, n=n_s, t=t_s, nh=b_heads, qcol=qcol, pp=8)

            xp = out_proj(xp, ya_p, ob_p, w_out, tm=tm_p, tn=512)
            xs = out_proj(xs, ya_s, ob_s, w_out, tm=tm_s, tn=512)

            k_p = zp[:, 2 * a_w + b_w:2 * a_w + 2 * b_w]
            v_p = zp[:, 2 * a_w + 2 * b_w:2 * a_w + 3 * b_w]
            st_p['fox_k'].append(k_p.reshape(n_p, s_len, b_heads, HD))
            st_p['fox_v'].append(v_p.reshape(n_p, s_len, b_heads, HD))
            st_p['fox_logf'].append(lf_p.reshape(n_p, s_len, b_heads))
            st_p['lru_conv'].append(cv_p)
            st_p['lru_h'].append(h_p.reshape(n_p, a_w))
            st_s['fox_k'].append(k_s.reshape(n_s, t_s, b_heads, HD))
            st_s['fox_v'].append(v_s.reshape(n_s, t_s, b_heads, HD))
            st_s['fox_logf'].append(lf_s.reshape(n_s, t_s, b_heads))
            st_s['lru_conv'].append(cv_s)
            st_s['lru_h'].append(h_s.reshape(n_s, a_w))
        else:
            w_in = odd_w_in[j]
            w_out = odd_w_out[j]
            qcol = 4 * c_w // d_w
            zp = norm_matmul(xp, g_mix, w_in, tm=tm_p, tn=512)
            zs = norm_matmul(xs, g_mix, w_in, tm=tm_s, tn=512)

            oc_p, s_p = hgrn_mixer(zp, jnp.zeros((n_p, c_heads, HD, HD), F32), hgrn_lb_logits, hgrn_norm[j],
                                   n=n_p, t=s_len, layer=l)
            od_p = dsw_prompt(zp, n=n_p, s_len=s_len, nkv=d_heads, qcol=qcol)
            oc_s, s_s = hgrn_sample(zs, state_hgrn, hgrn_lb_logits, hgrn_norm[j],
                                    s_layer=j, n=n_s, t=t_s, layer=l)
            od_s = dsw_sample(zs, cache_dsw_k, cache_dsw_v,
                              layer=j, n=n_s, t=t_s, nkv=d_heads, qcol=qcol, row0=0)

            xp = out_proj(xp, oc_p, od_p, w_out, tm=tm_p, tn=512)
            xs = out_proj(xs, oc_s, od_s, w_out, tm=tm_s, tn=512)

            k0 = 4 * c_w + len(D_GROUPS) * d_w
            zp3 = zp.reshape(n_p, s_len, -1)
            st_p['hgrn'].append(s_p)
            st_p['dsw_k'].append(zp3[:, s_len - keep:, k0:k0 + d_w].reshape(n_p, keep, d_heads, HD))
            st_p['dsw_v'].append(zp3[:, s_len - keep:, k0 + d_w:k0 + 2 * d_w].reshape(n_p, keep, d_heads, HD))
            st_s['hgrn'].append(s_s)
            st_s['dsw_k'].append(zs[:, k0:k0 + d_w].reshape(n_s, t_s, d_heads, HD))
            st_s['dsw_v'].append(zs[:, k0 + d_w:k0 + 2 * d_w].reshape(n_s, t_s, d_heads, HD))
        xp, xs = ffn_both(xp, xs, l, 1, l == depth - 1)

    y_prompt = xp.reshape(n_p, s_len, d)
    y_sample = xs.reshape(n_s, t_s, d)
    return (y_prompt, y_sample,
            *(jnp.stack(st_p[nm]) for nm in names),
            *(jnp.stack(st_s[nm]) for nm in names))
```

```python
import functools

import numpy as np
import jax
import jax.numpy as jnp
from jax import lax
from jax.experimental import pallas as pl
from jax.experimental.pallas import tpu as pltpu

F32 = jnp.float32
BF16 = jnp.bfloat16
EPS = 1e-6
HD = 128
LRU_C = 8.0
CONV_W = 4
D_GROUPS = ((128, 1), (512, 4), (2048, 16))
VMEM_LIMIT_BYTES = 58 * 1024 * 1024
HIGHEST = lax.Precision.HIGHEST
NT_DIMS = (((1,), (1,)), ((), ()))


def _params(*sem):
    return pltpu.CompilerParams(dimension_semantics=sem, vmem_limit_bytes=VMEM_LIMIT_BYTES)


def _rms(x, g):
    ms = jnp.mean(x * x, axis=-1, keepdims=True)
    return x * lax.rsqrt(ms + EPS) * g


def _softplus(z):
    return jnp.maximum(z, 0.0) + jnp.log1p(jnp.exp(-jnp.abs(z)))


def _log_sigmoid(z):
    return -_softplus(-z)


def _dot_nt(a, b):
    return lax.dot_general(a, b, NT_DIMS, preferred_element_type=F32)


def _col_from_row(row, eye):
    return jnp.sum(eye * row, axis=1, keepdims=True)


def _ffn_kernel(x_ref, g_ref, wg_ref, wu_ref, wd_ref, gf_ref, o_ref, xn_ref, *, nf, final):
    f = pl.program_id(1)

    @pl.when(f == 0)
    def _():
        xn_ref[...] = _rms(x_ref[...], g_ref[...]).astype(BF16)
        o_ref[...] = jnp.zeros_like(o_ref)

    xn = xn_ref[...]
    a = jnp.dot(xn, wg_ref[...].astype(BF16), preferred_element_type=F32)
    b = jnp.dot(xn, wu_ref[...].astype(BF16), preferred_element_type=F32)
    h = (a * jax.nn.sigmoid(a) * b).astype(BF16)
    o_ref[...] += jnp.dot(h, wd_ref[...].astype(BF16), preferred_element_type=F32)

    @pl.when(f == nf - 1)
    def _():
        y = x_ref[...] + 0.5 * o_ref[...]
        if final:
            y = _rms(y, gf_ref[...])
        o_ref[...] = y


def ffn(x, g, wg, wu, wd, gf, *, l, k, final, tm, tf):
    m, d = x.shape
    nf = wg.shape[-1] // tf
    return pl.pallas_call(
        functools.partial(_ffn_kernel, nf=nf, final=final),
        grid=(m // tm, nf),
        in_specs=[
            pl.BlockSpec((tm, d), lambda i, f: (i, 0)),
            pl.BlockSpec((1, d), lambda i, f: (0, 0)),
            pl.BlockSpec((None, None, d, tf), lambda i, f: (l, k, 0, f)),
            pl.BlockSpec((None, None, d, tf), lambda i, f: (l, k, 0, f)),
            pl.BlockSpec((None, None, tf, d), lambda i, f: (l, k, f, 0)),
            pl.BlockSpec((1, d), lambda i, f: (0, 0)),
        ],
        out_specs=pl.BlockSpec((tm, d), lambda i, f: (i, 0)),
        out_shape=jax.ShapeDtypeStruct((m, d), F32),
        scratch_shapes=[pltpu.VMEM((tm, d), BF16)],
        compiler_params=_params("parallel", "arbitrary"),
        name="ffn",
    )(x, g.reshape(1, d), wg, wu, wd, gf.reshape(1, d))


def _norm_matmul_kernel(x_ref, g_ref, w_ref, o_ref, xn_ref):
    @pl.when(pl.program_id(1) == 0)
    def _():
        xn_ref[...] = _rms(x_ref[...], g_ref[...]).astype(BF16)

    o_ref[...] = jnp.dot(xn_ref[...], w_ref[...].astype(BF16), preferred_element_type=F32)


def norm_matmul(x, g, w, *, tm, tn):
    m, d = x.shape
    n = w.shape[1] // tn * tn
    return pl.pallas_call(
        _norm_matmul_kernel,
        grid=(m // tm, n // tn),
        in_specs=[
            pl.BlockSpec((tm, d), lambda i, j: (i, 0)),
            pl.BlockSpec((1, d), lambda i, j: (0, 0)),
            pl.BlockSpec((d, tn), lambda i, j: (0, j)),
        ],
        out_specs=pl.BlockSpec((tm, tn), lambda i, j: (i, j)),
        out_shape=jax.ShapeDtypeStruct((m, n), F32),
        scratch_shapes=[pltpu.VMEM((tm, d), BF16)],
        compiler_params=_params("parallel", "arbitrary"),
        name="norm_matmul",
    )(x, g.reshape(1, d), w)


def _out_proj_kernel(r_ref, ya_ref, yb_ref, wa_ref, wb_ref, o_ref, ya_s, yb_s):
    @pl.when(pl.program_id(1) == 0)
    def _():
        ya_s[...] = ya_ref[...].astype(BF16)
        yb_s[...] = yb_ref[...].astype(BF16)

    acc = jnp.dot(ya_s[...], wa_ref[...].astype(BF16), preferred_element_type=F32)
    acc = acc + jnp.dot(yb_s[...], wb_ref[...].astype(BF16), preferred_element_type=F32)
    o_ref[...] = r_ref[...] + acc


def out_proj(res, ya, yb, w, *, tm, tn):
    m, d = res.shape
    ka, kb = ya.shape[1], yb.shape[1]
    assert ka % kb == 0
    return pl.pallas_call(
        _out_proj_kernel,
        grid=(m // tm, d // tn),
        in_specs=[
            pl.BlockSpec((tm, tn), lambda i, j: (i, j)),
            pl.BlockSpec((tm, ka), lambda i, j: (i, 0)),
            pl.BlockSpec((tm, kb), lambda i, j: (i, 0)),
            pl.BlockSpec((ka, tn), lambda i, j: (0, j)),
            pl.BlockSpec((kb, tn), lambda i, j: (ka // kb, j)),
        ],
        out_specs=pl.BlockSpec((tm, tn), lambda i, j: (i, j)),
        out_shape=jax.ShapeDtypeStruct((m, d), F32),
        scratch_shapes=[pltpu.VMEM((tm, ka), BF16), pltpu.VMEM((tm, kb), BF16)],
        compiler_params=_params("parallel", "arbitrary"),
        name="out_proj",
    )(res, ya, yb, w, w)


def _lru_kernel(ax_ref, ag_ref, cb0_ref, h0_ref, cw_ref, cb_ref, wri_ref, br_ref, bi_ref, lam_ref,
                ya_ref, cst_ref, hl_ref, xbuf, hcar, a_s, b_s, *, tt, nh, starts):
    tb = pl.program_id(1)

    @pl.when(tb == 0)
    def _():
        xbuf[5:8, :] = cb0_ref[...]
        hcar[...] = h0_ref[...]

    x = ax_ref[...]
    xbuf[8:8 + tt, :] = x
    cw = cw_ref[...]
    y = cb_ref[...] + xbuf[5:5 + tt, :] * cw[0:1, :]
    y = y + xbuf[6:6 + tt, :] * cw[1:2, :]
    y = y + xbuf[7:7 + tt, :] * cw[2:3, :]
    y = y + x * cw[3:4, :]
    tail = xbuf[5 + tt:8 + tt, :]
    xbuf[5:8, :] = tail
    cst_ref[...] = tail

    sp = _softplus(-lam_ref[...])
    for h in range(nh):
        sl = slice(h * HD, (h + 1) * HD)
        yh = y[:, sl]
        gts = jnp.dot(yh.astype(BF16), wri_ref[h], preferred_element_type=F32)
        gr = jax.nn.sigmoid(gts[:, :HD] + br_ref[:, sl])
        gi = jax.nn.sigmoid(gts[:, HD:] + bi_ref[:, sl])
        log_a = -LRU_C * gr * sp[:, sl]
        a = jnp.exp(log_a)
        mult = jnp.sqrt(-jnp.tanh(log_a) * (a * a + 1.0))
        if starts:
            row = lax.broadcasted_iota(jnp.int32, mult.shape, 0)
            mult = jnp.where((row == 0) & (tb == 0), 1.0, mult)
        a_s[:, sl] = a
        b_s[:, sl] = mult * gi * yh

    def group(gidx, hprev):
        r = pl.multiple_of(gidx * 8, 8)
        a8 = a_s[pl.ds(r, 8), :]
        b8 = b_s[pl.ds(r, 8), :]
        row = lax.broadcasted_iota(jnp.int32, a8.shape, 0)
        for s in (1, 2, 4):
            ok = row >= s
            b8 = jnp.where(ok, a8 * pltpu.roll(b8, s, 0) + b8, b8)
            a8 = jnp.where(ok, a8 * pltpu.roll(a8, s, 0), a8)
        h8 = a8 * hprev + b8
        b_s[pl.ds(r, 8), :] = h8
        return h8[7:8, :]

    hlast = lax.fori_loop(0, tt // 8, group, hcar[...])
    hcar[...] = hlast
    hl_ref[...] = hlast
    ya_ref[...] = b_s[...] * jax.nn.gelu(ag_ref[...])


def lru_mixer(z, conv0, h0, cw, cb, wri, br, bi, lam, *, n, t, tt, starts):
    w = lam.shape[-1]
    nh = w // HD
    nb = t // tt
    row = lambda i, b: (i * nb + b, 0)
    vec = pl.BlockSpec((1, w), lambda i, b: (0, 0))
    return pl.pallas_call(
        functools.partial(_lru_kernel, tt=tt, nh=nh, starts=starts),
        grid=(n, nb),
        in_specs=[
            pl.BlockSpec((tt, w), row),
            pl.BlockSpec((tt, w), lambda i, b: (i * nb + b, 1)),
            pl.BlockSpec((None, CONV_W - 1, w), lambda i, b: (i, 0, 0)),
            pl.BlockSpec((None, 1, w), lambda i, b: (i, 0, 0)),
            pl.BlockSpec((CONV_W, w), lambda i, b: (0, 0)),
            vec,
            pl.BlockSpec((nh, HD, 2 * HD), lambda i, b: (0, 0, 0)),
            vec, vec, vec,
        ],
        out_specs=[
            pl.BlockSpec((tt, w), row),
            pl.BlockSpec((None, CONV_W - 1, w), lambda i, b: (i, 0, 0)),
            pl.BlockSpec((None, 1, w), lambda i, b: (i, 0, 0)),
        ],
        out_shape=[
            jax.ShapeDtypeStruct((n * t, w), F32),
            jax.ShapeDtypeStruct((n, CONV_W - 1, w), F32),
            jax.ShapeDtypeStruct((n, 1, w), F32),
        ],
        scratch_shapes=[
            pltpu.VMEM((tt + 8, w), F32),
            pltpu.VMEM((1, w), F32),
            pltpu.VMEM((tt, w), F32),
            pltpu.VMEM((tt, w), F32),
        ],
        compiler_params=_params("parallel", "arbitrary"),
        name="lru_mixer",
    )(z, z, conv0, h0.reshape(n, 1, w), cw, cb.reshape(1, w), wri,
      br.reshape(1, w), bi.reshape(1, w), lam.reshape(1, w))


def _fox_prep_kernel(zf_ref, bf_ref, lf_ref, cum_ref, cumt_ref, cum_s, cumt_s, *, s_len, nh):
    lf = _log_sigmoid(zf_ref[...] + bf_ref[...])
    lf_ref[...] = lf[:, :nh]
    cum_s[...] = lf
    r = lax.broadcasted_iota(jnp.int32, (HD, HD), 0)
    c = lax.broadcasted_iota(jnp.int32, (HD, HD), 1)
    tri = (r >= c).astype(F32)

    def chunk(i, carry):
        st = pl.multiple_of(i * HD, HD)
        cs = jnp.dot(tri, cum_s[pl.ds(st, HD), :], preferred_element_type=F32, precision=HIGHEST) + carry
        cum_s[pl.ds(st, HD), :] = cs
        cumt_s[:, pl.ds(st, HD)] = cs.T
        return cs[HD - 1:HD, :]

    lax.fori_loop(0, s_len // HD, chunk, jnp.zeros((1, HD), F32))
    cum_ref[...] = cum_s[:, :nh]
    cumt_ref[...] = cumt_s[:nh, :]


def fox_prep(zf, bfp, *, n, s_len, nh):
    return pl.pallas_call(
        functools.partial(_fox_prep_kernel, s_len=s_len, nh=nh),
        grid=(n,),
        in_specs=[
            pl.BlockSpec((s_len, HD), lambda i: (i, 0)),
            pl.BlockSpec((1, HD), lambda i: (0, 0)),
        ],
        out_specs=[
            pl.BlockSpec((s_len, nh), lambda i: (i, 0)),
            pl.BlockSpec((s_len, nh), lambda i: (i, 0)),
            pl.BlockSpec((None, nh, s_len), lambda i: (i, 0, 0)),
        ],
        out_shape=[
            jax.ShapeDtypeStruct((n * s_len, nh), F32),
            jax.ShapeDtypeStruct((n * s_len, nh), F32),
            jax.ShapeDtypeStruct((n, nh, s_len), F32),
        ],
        scratch_shapes=[pltpu.VMEM((s_len, HD), F32), pltpu.VMEM((HD, s_len), F32)],
        compiler_params=_params("parallel"),
        name="fox_prep",
    )(zf, bfp)


LOG2E = 1.4426950408889634


def _fox_prompt_kernel(q_ref, k_ref, v_ref, cq_ref, ck_ref, o_ref, m_s, l_s, acc_s, *, tq, nh, scale):
    qi = pl.program_id(1)
    kj = pl.program_id(2)

    @pl.when(kj == 0)
    def _():
        m_s[...] = jnp.full_like(m_s, -jnp.inf)
        l_s[...] = jnp.zeros_like(l_s)
        acc_s[...] = jnp.zeros_like(acc_s)

    def block(causal):
        for h in range(nh):
            sl = slice(h * HD, (h + 1) * HD)
            u = _dot_nt(k_ref[:, sl].astype(BF16), q_ref[:, sl].astype(BF16)) * (scale * LOG2E)
            u = u - ck_ref[:, h:h + 1] * LOG2E
            if causal is not None:
                u = jnp.where(causal, u, -jnp.inf)
            cq = cq_ref[h:h + 1, :] * LOG2E
            m_prev = m_s[h]
            m_new = jnp.maximum(m_prev, jnp.max(u, axis=0, keepdims=True) + cq)
            alpha = jnp.exp2(m_prev - m_new)
            p = jnp.exp2(u - (m_new - cq))
            l_s[h] = alpha * l_s[h] + jnp.sum(p, axis=0, keepdims=True)
            acc_s[sl, :] = alpha * acc_s[sl, :] + jnp.dot(
                v_ref[:, sl].T.astype(BF16), p.astype(BF16), preferred_element_type=F32)
            m_s[h] = m_new

    @pl.when(kj < qi)
    def _():
        block(None)

    @pl.when(kj == qi)
    def _():
        key = lax.broadcasted_iota(jnp.int32, (tq, tq), 0)
        qry = lax.broadcasted_iota(jnp.int32, (tq, tq), 1)
        block(key <= qry)
        for h in range(nh):
            sl = slice(h * HD, (h + 1) * HD)
            o_ref[:, sl] = (acc_s[sl, :] / l_s[h]).T


def fox_prompt(z, cum, cumt, *, n, s_len, tq, qcol):
    nh = cum.shape[1]
    w = nh * HD
    nq = s_len // tq
    return pl.pallas_call(
        functools.partial(_fox_prompt_kernel, tq=tq, nh=nh, scale=HD ** -0.5),
        grid=(n, nq, nq),
        in_specs=[
            pl.BlockSpec((tq, w), lambda i, a, b: (i * nq + a, qcol)),
            pl.BlockSpec((tq, w), lambda i, a, b: (i * nq + jnp.minimum(a, b), qcol + 1)),
            pl.BlockSpec((tq, w), lambda i, a, b: (i * nq + jnp.minimum(a, b), qcol + 2)),
            pl.BlockSpec((None, nh, tq), lambda i, a, b: (i, 0, a)),
            pl.BlockSpec((tq, nh), lambda i, a, b: (i * nq + jnp.minimum(a, b), 0)),
        ],
        out_specs=pl.BlockSpec((tq, w), lambda i, a, b: (i * nq + a, 0)),
        out_shape=jax.ShapeDtypeStruct((n * s_len, w), F32),
        scratch_shapes=[
            pltpu.VMEM((nh, 1, tq), F32),
            pltpu.VMEM((nh, 1, tq), F32),
            pltpu.VMEM((w, tq), F32),
        ],
        compiler_params=_params("parallel", "parallel", "arbitrary"),
        name="fox_prompt",
    )(z, z, z, cumt, cum)


def _split3(x):
    hi = x.astype(BF16)
    r1 = x - hi.astype(F32)
    mid = r1.astype(BF16)
    return hi, mid, (r1 - mid.astype(F32)).astype(BF16)


def _fox_sample_kernel(pt_ref, q_ref, kn_ref, vn_ref, lfn_ref, *refs, pp, t, nh, scale):
    kc, vc, lc = refs[0:pp], refs[pp:2 * pp], refs[2 * pp:3 * pp]
    o_ref = refs[3 * pp]
    qb_s, crow_s, carry_s, m_s, l_s, acc_s = refs[3 * pp + 1:]
    j = pl.program_id(1)
    nj = pl.num_programs(1)
    ncol = nh * t

    r128 = lax.broadcasted_iota(jnp.int32, (HD, HD), 0)
    c128 = lax.broadcasted_iota(jnp.int32, (HD, HD), 1)
    eye = (r128 == c128).astype(F32)
    later_in_page = (c128 > r128).astype(BF16)
    hrow = lax.broadcasted_iota(jnp.int32, (nh, HD), 0)
    hcol = lax.broadcasted_iota(jnp.int32, (nh, HD), 1)
    expand = ((hcol // t == hrow) & (hcol < ncol)).astype(BF16)

    def to_columns(lf):
        return [jnp.dot(part, expand, preferred_element_type=F32).astype(BF16) for part in _split3(lf)]

    def block(k2d, v2d, lf, valid):
        parts = jnp.concatenate(to_columns(lf), axis=1)
        carry = carry_s[...]
        crow = crow_s[...]
        later = []
        for i in reversed(range(k2d.shape[0] // HD)):
            page = parts[i * HD:(i + 1) * HD]
            within = jnp.dot(later_in_page, page, preferred_element_type=F32)
            later.append(within[:, :HD] + within[:, HD:2 * HD] + within[:, 2 * HD:] + (carry + crow))
            tot = jnp.sum(page.astype(F32), axis=0, keepdims=True)
            carry = carry + (tot[:, :HD] + tot[:, HD:2 * HD] + tot[:, 2 * HD:])
        carry_s[...] = carry
        later = later[0] if len(later) == 1 else jnp.concatenate(later[::-1], axis=0)
        s = _dot_nt(k2d, qb_s[...]) * scale + later
        if valid is not None:
            s = jnp.where(valid, s, -jnp.inf)
        m_prev = m_s[...]
        m_new = jnp.maximum(m_prev, jnp.max(s, axis=0, keepdims=True))
        alpha = jnp.exp(m_prev - m_new)
        p = jnp.exp(s - m_new)
        l_s[...] = alpha * l_s[...] + jnp.sum(p, axis=0, keepdims=True)
        m_s[...] = m_new
        acc_s[...] = _col_from_row(alpha, eye)[:ncol] * acc_s[...] + jnp.dot(
            p.T[:ncol].astype(BF16), v2d, preferred_element_type=F32)

    def heads_to_lanes(ref):
        return jnp.concatenate([ref[pl.ds(h, HD, stride=nh), :] for h in range(nh)], axis=1).astype(BF16)

    @pl.when(j == 0)
    def _():
        q = q_ref[...]
        w = q.shape[1]
        qrep = jnp.concatenate([q] * nh + [jnp.zeros((HD - ncol, w), F32)], axis=0)
        rr = lax.broadcasted_iota(jnp.int32, (HD, w), 0)
        cc = lax.broadcasted_iota(jnp.int32, (HD, w), 1)
        qb_s[...] = jnp.where(rr // t == cc // HD, qrep, 0.0).astype(BF16)
        m_s[...] = jnp.full_like(m_s, -jnp.inf)
        l_s[...] = jnp.zeros_like(l_s)
        acc_s[...] = jnp.zeros_like(acc_s)
        carry_s[...] = jnp.zeros_like(carry_s)
        lfn = lfn_ref[...]
        lfn_cols = functools.reduce(lambda a, b: a + b, [p.astype(F32) for p in to_columns(lfn)])
        after = r128 > (c128 % t)
        crow_s[...] = -jnp.sum(jnp.where(after, lfn_cols, 0.0), axis=0, keepdims=True)
        block(kn_ref[...].astype(BF16), vn_ref[...].astype(BF16), lfn, r128 <= (c128 % t))

    block(jnp.concatenate([heads_to_lanes(r) for r in kc], axis=0),
          jnp.concatenate([heads_to_lanes(r) for r in vc], axis=0),
          jnp.concatenate([r[...] for r in lc], axis=0), None)

    @pl.when(j == nj - 1)
    def _():
        lcol = _col_from_row(l_s[...], eye)
        for h in range(nh):
            o_ref[:, h * HD:(h + 1) * HD] = (
                acc_s[h * t:(h + 1) * t, h * HD:(h + 1) * HD] / lcol[h * t:(h + 1) * t, :])


def fox_sample(page_table, zs, kn_pad, vn_pad, lfn_pad, cache_k, cache_v, cache_lf, *, layer, n, t, nh, qcol, pp):
    w = nh * HD
    n_layers, n_pool = cache_k.shape[:2]
    cache_k = cache_k.reshape(n_layers, n_pool, HD * nh, HD)
    cache_v = cache_v.reshape(n_layers, n_pool, HD * nh, HD)
    n_pages = page_table.shape[1]
    nj = n_pages // pp

    def pg(i, tail):
        return lambda b, j, pt: (layer, pt[b, n_pages - (j + 1) * pp + i]) + tail

    in_specs = [
        pl.BlockSpec((t, w), lambda b, j, pt: (b, qcol)),
        pl.BlockSpec((None, HD, w), lambda b, j, pt: (b, 0, 0)),
        pl.BlockSpec((None, HD, w), lambda b, j, pt: (b, 0, 0)),
        pl.BlockSpec((None, HD, nh), lambda b, j, pt: (b, 0, 0)),
    ]
    in_specs += [pl.BlockSpec((None, None, HD * nh, HD), pg(i, (0, 0))) for i in range(pp)]
    in_specs += [pl.BlockSpec((None, None, HD * nh, HD), pg(i, (0, 0))) for i in range(pp)]
    in_specs += [pl.BlockSpec((None, None, HD, nh), pg(i, (0, 0))) for i in range(pp)]
    return pl.pallas_call(
        functools.partial(_fox_sample_kernel, pp=pp, t=t, nh=nh, scale=HD ** -0.5),
        grid_spec=pltpu.PrefetchScalarGridSpec(
            num_scalar_prefetch=1,
            grid=(n, nj),
            in_specs=in_specs,
            out_specs=pl.BlockSpec((t, w), lambda b, j, pt: (b, 0)),
            scratch_shapes=[
                pltpu.VMEM((HD, w), BF16),
                pltpu.VMEM((1, HD), F32),
                pltpu.VMEM((1, HD), F32),
                pltpu.VMEM((1, HD), F32),
                pltpu.VMEM((1, HD), F32),
                pltpu.VMEM((nh * t, w), F32),
            ],
        ),
        out_shape=jax.ShapeDtypeStruct((n * t, w), F32),
        compiler_params=_params("parallel", "arbitrary"),
        name="fox_sample",
    )(page_table, zs, kn_pad, vn_pad, lfn_pad, *([cache_k] * pp), *([cache_v] * pp), *([cache_lf] * pp))


HG_CHUNK = 128
HG_MATMUL_LEVELS = 3


def _hgrn_sum_matrix(seq_rows):
    c = HG_CHUNK
    t = np.arange(c)[:, None]
    i = np.arange(c)[None, :]
    same = (t // seq_rows) == (i // seq_rows)
    blocks = [same & (i <= t), same & (i > t)]
    for lv in range(1, HG_MATMUL_LEVELS + 1):
        s = 1 << lv
        half = s // 2
        start = (t // s) * s
        upper = (t % s) >= half
        blocks.append(np.where(upper, (i >= start + half) & (i <= t), (i > t) & (i <= start + half - 1)))
    d = np.concatenate(blocks, axis=0).astype(np.float32)
    return np.concatenate([d, d, d], axis=1)


def _hgrn_lower_bound(lbl_ref, layer):
    lg = lbl_ref[...]
    e = jnp.exp(lg - jnp.max(lg, axis=0, keepdims=True))
    sm = e / jnp.sum(e, axis=0, keepdims=True)
    return jnp.sum(sm[1:layer + 1, :], axis=0, keepdims=True)


def _hgrn_gates(cf, lb):
    e = jnp.exp(-jnp.abs(cf))
    inv = 1.0 / (1.0 + e)
    pos = cf >= 0.0
    sig = jnp.where(pos, inv, e * inv)
    sig_neg = jnp.where(pos, e * inv, inv)
    return jnp.log(lb + (1.0 - lb) * sig), (1.0 - lb) * sig_neg


def _hgrn_attention(q, kk, g, dm, levels):
    c = HG_CHUNK
    sums = jnp.dot(dm, jnp.concatenate(_split3(g), axis=0), preferred_element_type=F32)
    b = sums[0:c]
    suffix = sums[c:2 * c]
    row = lax.broadcasted_iota(jnp.int32, (c, HD), 0)
    rr = lax.broadcasted_iota(jnp.int32, (c, c), 0)
    cc = lax.broadcasted_iota(jnp.int32, (c, c), 1)
    att = jnp.where(rr == cc, _dot_nt(q.astype(BF16), kk.astype(BF16)), 0.0)
    for lv in range(1, levels + 1):
        s = 1 << lv
        half = s // 2
        upper = (row & (s - 1)) >= half
        if lv <= HG_MATMUL_LEVELS:
            d = sums[(1 + lv) * c:(2 + lv) * c]
        else:
            mids = [jnp.broadcast_to(b[k * s + half - 1:k * s + half, :], (s, HD)) for k in range(c // s)]
            bmid = mids[0] if len(mids) == 1 else jnp.concatenate(mids, axis=0)
            d = jnp.where(upper, b - bmid, bmid - b)
        e = jnp.exp(d)
        qt = (q * jnp.where(upper, e, 0.0)).astype(BF16)
        kt = (kk * jnp.where(upper, 0.0, e)).astype(BF16)
        att = att + jnp.where((rr >> lv) == (cc >> lv), _dot_nt(qt, kt), 0.0)
    return att, b, suffix


def _hgrn_kernel(q_ref, f_ref, i_ref, g_ref, s0_ref, lbl_ref, gn_ref, dm_ref, y_ref, so_ref, st_s, *, nh, layer):
    c = HG_CHUNK

    @pl.when(pl.program_id(1) == 0)
    def _():
        st_s[...] = s0_ref[...]

    lb = _hgrn_lower_bound(lbl_ref, layer)
    rr = lax.broadcasted_iota(jnp.int32, (c, c), 0)
    cc = lax.broadcasted_iota(jnp.int32, (c, c), 1)
    eye = (rr == cc).astype(F32)
    dm = dm_ref[...]

    for h in range(nh):
        sl = slice(h * HD, (h + 1) * HD)
        q, v = q_ref[:, sl], i_ref[:, sl]
        g, kk = _hgrn_gates(f_ref[:, sl], lb[:, sl])
        att, b, suffix = _hgrn_attention(q, kk, g, dm, c.bit_length() - 1)
        state = st_s[h]
        vb = v.astype(BF16)
        o = jnp.dot((q * jnp.exp(b)).astype(BF16), state.astype(BF16), preferred_element_type=F32)
        o = o + jnp.dot(att.astype(BF16), vb, preferred_element_type=F32)
        kd = kk * jnp.exp(suffix)
        ecol = _col_from_row(jnp.exp(b[c - 1:c, :]), eye)
        st_s[h] = ecol * state + jnp.dot(kd.T.astype(BF16), vb, preferred_element_type=F32)
        gate = g_ref[:, sl]
        y_ref[:, sl] = _rms(o, gn_ref[...]) * (gate * jax.nn.sigmoid(gate))

    so_ref[...] = st_s[...]


def _hgrn_sample_kernel(q_ref, f_ref, i_ref, g_ref, s0_ref, lbl_ref, gn_ref, dm_ref, y_ref, so_ref,
                        qe_s, kdt_s, e_s, o_s, *, t, nh, layer):
    c = HG_CHUNK
    lb = _hgrn_lower_bound(lbl_ref, layer)
    rr = lax.broadcasted_iota(jnp.int32, (c, c), 0)
    cc = lax.broadcasted_iota(jnp.int32, (c, c), 1)
    eye = (rr == cc).astype(F32)
    dm = dm_ref[...]

    for h in range(nh):
        sl = slice(h * HD, (h + 1) * HD)
        q, v = q_ref[:, sl], i_ref[:, sl]
        g, kk = _hgrn_gates(f_ref[:, sl], lb[:, sl])
        att, b, suffix = _hgrn_attention(q, kk, g, dm, t.bit_length() - 1)
        vb = v.astype(BF16)
        e = jnp.exp(b)
        e_s[...] = e
        qe_s[...] = q * e
        kdt_s[...] = (kk * jnp.exp(suffix)).T
        o_s[...] = jnp.dot(att.astype(BF16), vb, preferred_element_type=F32)

        def sequence(n, carry):
            r = pl.multiple_of(n * t, t)
            state = s0_ref[n, h]
            o_s[pl.ds(r, t), :] += jnp.dot(qe_s[pl.ds(r, t), :].astype(BF16), state.astype(BF16),
                                           preferred_element_type=F32)
            ecol = _col_from_row(e_s[pl.ds(r, t), :][t - 1:t, :], eye)
            own = (cc >= r) & (cc < r + t)
            kdt = jnp.where(own, kdt_s[...], 0.0).astype(BF16)
            so_ref[n, h] = ecol * state + jnp.dot(kdt, vb, preferred_element_type=F32)
            return carry

        lax.fori_loop(0, c // t, sequence, 0)
        gate = g_ref[:, sl]
        y_ref[:, sl] = _rms(o_s[...], gn_ref[...]) * (gate * jax.nn.sigmoid(gate))


def hgrn_mixer(z, s0, lb_logits, g_norm, *, n, t, layer):
    nh = s0.shape[1]
    w = nh * HD
    c = HG_CHUNK
    assert t % c == 0
    nc = t // c
    dm = jnp.asarray(_hgrn_sum_matrix(c), BF16)
    col = lambda k: pl.BlockSpec((c, w), lambda i, b: (i * nc + b, k))
    return pl.pallas_call(
        functools.partial(_hgrn_kernel, nh=nh, layer=layer),
        grid=(n, nc),
        in_specs=[
            col(0), col(1), col(2), col(3),
            pl.BlockSpec((None, nh, HD, HD), lambda i, b: (i, 0, 0, 0)),
            pl.BlockSpec(lb_logits.shape, lambda i, b: (0, 0)),
            pl.BlockSpec((1, HD), lambda i, b: (0, 0)),
            pl.BlockSpec(dm.shape, lambda i, b: (0, 0)),
        ],
        out_specs=[
            col(0),
            pl.BlockSpec((None, nh, HD, HD), lambda i, b: (i, 0, 0, 0)),
        ],
        out_shape=[
            jax.ShapeDtypeStruct((n * t, w), F32),
            jax.ShapeDtypeStruct((n, nh, HD, HD), F32),
        ],
        scratch_shapes=[pltpu.VMEM((nh, HD, HD), F32)],
        compiler_params=_params("parallel", "arbitrary"),
        name="hgrn_mixer",
    )(z, z, z, z, s0, lb_logits, g_norm.reshape(1, HD), dm)


def hgrn_sample(z, s0, lb_logits, g_norm, *, s_layer, n, t, layer):
    nh = s0.shape[2]
    w = nh * HD
    c = HG_CHUNK
    assert c % t == 0 and t % 8 == 0 and t & (t - 1) == 0 and (n * t) % c == 0
    nseq = c // t
    dm = jnp.asarray(_hgrn_sum_matrix(t), BF16)
    col = lambda k: pl.BlockSpec((c, w), lambda i: (i, k))
    return pl.pallas_call(
        functools.partial(_hgrn_sample_kernel, t=t, nh=nh, layer=layer),
        grid=(n // nseq,),
        in_specs=[
            col(0), col(1), col(2), col(3),
            pl.BlockSpec((None, nseq, nh, HD, HD), lambda i: (s_layer, i, 0, 0, 0)),
            pl.BlockSpec(lb_logits.shape, lambda i: (0, 0)),
            pl.BlockSpec((1, HD), lambda i: (0, 0)),
            pl.BlockSpec(dm.shape, lambda i: (0, 0)),
        ],
        out_specs=[
            col(0),
            pl.BlockSpec((nseq, nh, HD, HD), lambda i: (i, 0, 0, 0)),
        ],
        out_shape=[
            jax.ShapeDtypeStruct((n * t, w), F32),
            jax.ShapeDtypeStruct((n, nh, HD, HD), F32),
        ],
        scratch_shapes=[pltpu.VMEM((c, HD), F32)] * 4,
        compiler_params=_params("parallel"),
        name="hgrn_sample",
    )(z, z, z, z, s0, lb_logits, g_norm.reshape(1, HD), dm)


def _merge_groups(outs, lses):
    mx = functools.reduce(jnp.maximum, lses)
    ws = [jnp.exp(l - mx) for l in lses]
    den = functools.reduce(lambda a, b: a + b, ws)
    return functools.reduce(lambda a, b: a + b, [w * o for w, o in zip(ws, outs)]) / den


DSW_BLOCK = 128


def _dsw_prompt_kernel(q0_ref, q1_ref, q2_ref, k_ref, v_ref, o_ref, og_s, lse_s, *, s_len, scale):
    qrefs = (q0_ref, q1_ref, q2_ref)
    qb = DSW_BLOCK
    j = lax.broadcasted_iota(jnp.int32, (2 * qb, qb), 0)
    i = lax.broadcasted_iota(jnp.int32, (2 * qb, qb), 1)
    for g, (win, dil) in enumerate(D_GROUPS):
        band = win // dil
        assert band <= qb and (s_len // dil) % qb == 0
        band_mask = (j >= i + qb - band) & (j <= i + qb)
        first_mask = band_mask[qb:, :]
        for r in range(dil):
            for lb in range(s_len // dil // qb):
                rows = pl.ds(r + lb * qb * dil, qb, stride=dil)
                q = qrefs[g][rows, :].astype(BF16)
                if lb == 0:
                    keys, mask = rows, first_mask
                else:
                    keys, mask = pl.ds(r + (lb - 1) * qb * dil, 2 * qb, stride=dil), band_mask
                s = jnp.where(mask, _dot_nt(k_ref[keys, :].astype(BF16), q) * scale, -jnp.inf)
                m = jnp.max(s, axis=0, keepdims=True)
                p = jnp.exp(s - m)
                l = jnp.sum(p, axis=0, keepdims=True)
                o = jnp.dot(v_ref[keys, :].T.astype(BF16), p.astype(BF16), preferred_element_type=F32)
                og_s[g, rows, :] = (o / l).T
                lse_s[g, rows, :] = jnp.broadcast_to(m + jnp.log(l), (HD, qb)).T
    ng = len(D_GROUPS)
    o_ref[...] = _merge_groups([og_s[g] for g in range(ng)], [lse_s[g] for g in range(ng)])


def dsw_prompt(z, *, n, s_len, nkv, qcol):
    ng = len(D_GROUPS)
    c0 = qcol * nkv
    spec = lambda k: pl.BlockSpec((s_len, HD), lambda b, h: (b, c0 + k * nkv + h))
    return pl.pallas_call(
        functools.partial(_dsw_prompt_kernel, s_len=s_len, scale=HD ** -0.5),
        grid=(n, nkv),
        in_specs=[spec(0), spec(1), spec(2), spec(ng), spec(ng + 1)],
        out_specs=pl.BlockSpec((s_len, HD), lambda b, h: (b, h)),
        out_shape=jax.ShapeDtypeStruct((n * s_len, nkv * HD), F32),
        scratch_shapes=[pltpu.VMEM((ng, s_len, HD), F32), pltpu.VMEM((ng, s_len, HD), F32)],
        compiler_params=_params("parallel", "parallel"),
        name="dsw_prompt",
    )(z, z, z, z, z)


def _dsw_sample_kernel(q0_ref, q1_ref, q2_ref, kn_ref, vn_ref, kc_ref, vc_ref, o_ref, kn_s, vn_s,
                       *, t, nkv, wbuf, scale):
    ng = len(D_GROUPS)
    kn_s[...] = jnp.zeros_like(kn_s)
    vn_s[...] = jnp.zeros_like(vn_s)
    kn_s[0:t, :] = kn_ref[...]
    vn_s[0:t, :] = vn_ref[...]

    def group_consts(shape):
        r = lax.broadcasted_iota(jnp.int32, shape, 0)
        grp = r // t
        win = jnp.zeros(shape, jnp.int32)
        dmask = jnp.zeros(shape, jnp.int32)
        for g, (w_, d_) in enumerate(D_GROUPS):
            win = jnp.where(grp == g, w_, win)
            dmask = jnp.where(grp == g, d_ - 1, dmask)
        return r % t, win, dmask

    tok_c, win_c, dm_c = group_consts((ng * t, wbuf))
    dist_c = wbuf + tok_c - lax.broadcasted_iota(jnp.int32, (ng * t, wbuf), 1)
    valid_c = (dist_c >= 0) & (dist_c <= win_c) & ((dist_c & dm_c) == 0)
    tok_n, win_n, dm_n = group_consts((ng * t, HD))
    dist_n = tok_n - lax.broadcasted_iota(jnp.int32, (ng * t, HD), 1)
    valid_n = (dist_n >= 0) & (dist_n <= win_n) & ((dist_n & dm_n) == 0)

    for h in range(nkv):
        sl = slice(h * HD, (h + 1) * HD)
        q = jnp.concatenate([q0_ref[:, sl], q1_ref[:, sl], q2_ref[:, sl]], axis=0).astype(BF16)
        kc = kc_ref[pl.ds(h, wbuf, stride=nkv), :].astype(BF16)
        vc = vc_ref[pl.ds(h, wbuf, stride=nkv), :].astype(BF16)
        s_c = jnp.where(valid_c, _dot_nt(q, kc) * scale, -jnp.inf)
        s_n = jnp.where(valid_n, _dot_nt(q, kn_s[:, sl].astype(BF16)) * scale, -jnp.inf)
        m = jnp.maximum(jnp.max(s_c, axis=-1, keepdims=True), jnp.max(s_n, axis=-1, keepdims=True))
        p_c = jnp.exp(s_c - m)
        p_n = jnp.exp(s_n - m)
        l = jnp.sum(p_c, axis=-1, keepdims=True) + jnp.sum(p_n, axis=-1, keepdims=True)
        o = jnp.dot(p_c.astype(BF16), vc, preferred_element_type=F32)
        o = (o + jnp.dot(p_n.astype(BF16), vn_s[:, sl].astype(BF16), preferred_element_type=F32)) / l
        lse = m + jnp.log(l)
        o_ref[:, sl] = _merge_groups([o[g * t:(g + 1) * t] for g in range(ng)],
                                     [lse[g * t:(g + 1) * t] for g in range(ng)])


def dsw_sample(z, cache_k, cache_v, *, layer, n, t, nkv, qcol, row0):
    w = nkv * HD
    n_layers, _, wbuf = cache_k.shape[:3]
    cache_k = cache_k.reshape(n_layers, n, wbuf * nkv, HD)
    cache_v = cache_v.reshape(n_layers, n, wbuf * nkv, HD)
    ng = len(D_GROUPS)
    rb0 = row0 // t
    col = lambda k: pl.BlockSpec((t, w), lambda i: (rb0 + i, qcol + k))
    cache = pl.BlockSpec((None, None, wbuf * nkv, HD), lambda i: (layer, i, 0, 0))
    return pl.pallas_call(
        functools.partial(_dsw_sample_kernel, t=t, nkv=nkv, wbuf=wbuf, scale=HD ** -0.5),
        grid=(n,),
        in_specs=[col(0), col(1), col(2), col(ng), col(ng + 1), cache, cache],
        out_specs=pl.BlockSpec((t, w), lambda i: (i, 0)),
        out_shape=jax.ShapeDtypeStruct((n * t, w), F32),
        scratch_shapes=[pltpu.VMEM((HD, w), F32), pltpu.VMEM((HD, w), F32)],
        compiler_params=_params("parallel"),
        name="dsw_sample",
    )(z, z, z, z, z, cache_k, cache_v)


def kernel(x_prompt, x_sample, cache_fox_k, cache_fox_v, cache_fox_logf, page_table, state_lru_conv, state_lru_h, state_hgrn, cache_dsw_k, cache_dsw_v, norm_gains, ffn_w_gate, ffn_w_up, ffn_w_down, even_w_in, even_w_out, lru_conv_w, lru_conv_b, lru_w_r, lru_b_r, lru_w_i, lru_b_i, lru_lambda, fox_b_f, odd_w_in, odd_w_out, hgrn_lb_logits, hgrn_norm, final_norm):
    n_p, s_len, d = x_prompt.shape
    n_s, t_s, _ = x_sample.shape
    depth = norm_gains.shape[0]
    a_w = lru_lambda.shape[-1]
    b_heads = fox_b_f.shape[-1]
    b_w = b_heads * HD
    c_heads = state_hgrn.shape[2]
    c_w = c_heads * HD
    d_heads = cache_dsw_k.shape[3]
    d_w = d_heads * HD
    n_pool = cache_fox_k.shape[1]
    wbuf = cache_dsw_k.shape[2]
    keep = min(max(w for w, _ in D_GROUPS), s_len)

    xp = x_prompt.reshape(n_p * s_len, d)
    xs = x_sample.reshape(n_s * t_s, d)
    tm_p, tm_s = 1024, n_s * t_s
    names = ('fox_k', 'fox_v', 'fox_logf', 'lru_conv', 'lru_h', 'hgrn', 'dsw_k', 'dsw_v')
    st_p = {nm: [] for nm in names}
    st_s = {nm: [] for nm in names}

    def ffn_both(xp, xs, l, k, final):
        g = norm_gains[l, 2 * k]
        w3 = (ffn_w_gate, ffn_w_up, ffn_w_down)
        return (ffn(xp, g, *w3, final_norm, l=l, k=k, final=final, tm=tm_p, tf=256),
                ffn(xs, g, *w3, final_norm, l=l, k=k, final=final, tm=tm_s, tf=512))

    for l in range(depth):
        j = l // 2
        xp, xs = ffn_both(xp, xs, l, 0, False)
        g_mix = norm_gains[l, 1]
        if l % 2 == 0:
            n_main = 2 * a_w + 3 * b_w
            w_in = even_w_in[j]
            w_f = jnp.pad(w_in[:, n_main:], ((0, 0), (0, HD - b_heads)))
            bfp = jnp.pad(fox_b_f[j], (0, HD - b_heads)).reshape(1, HD)
            wri = jnp.concatenate([lru_w_r[j], lru_w_i[j]], axis=-1).astype(BF16)
            w_out = even_w_out[j]
            lru_args = (lru_conv_w[j], lru_conv_b[j], wri, lru_b_r[j], lru_b_i[j], lru_lambda[j])
            qcol = 2 * a_w // b_w

            zp = norm_matmul(xp, g_mix, w_in, tm=tm_p, tn=1024)
            zfp = norm_matmul(xp, g_mix, w_f, tm=tm_p, tn=HD)
            zs = norm_matmul(xs, g_mix, w_in, tm=tm_s, tn=1024)
            zfs = norm_matmul(xs, g_mix, w_f, tm=tm_s, tn=HD)

            ya_p, cv_p, h_p = lru_mixer(zp, jnp.zeros((n_p, CONV_W - 1, a_w), F32), jnp.zeros((n_p, a_w), F32),
                                        *lru_args, n=n_p, t=s_len, tt=256, starts=True)
            lf_p, cum_p, cumt_p = fox_prep(zfp, bfp, n=n_p, s_len=s_len, nh=b_heads)
            ob_p = fox_prompt(zp, cum_p, cumt_p, n=n_p, s_len=s_len, tq=512, qcol=qcol)

            ya_s, cv_s, h_s = lru_mixer(zs, state_lru_conv[j], state_lru_h[j],
                                        *lru_args, n=n_s, t=t_s, tt=t_s, starts=False)
            lf_s, _, _ = fox_prep(zfs, bfp, n=1, s_len=n_s * t_s, nh=b_heads)
            k_s = zs[:, 2 * a_w + b_w:2 * a_w + 2 * b_w]
            v_s = zs[:, 2 * a_w + 2 * b_w:2 * a_w + 3 * b_w]
            pad_rows = lambda a: jnp.pad(a.reshape(n_s, t_s, -1), ((0, 0), (0, HD - t_s), (0, 0)))
            ob_s = fox_sample(page_table, zs, pad_rows(k_s), pad_rows(v_s), pad_rows(lf_s),
                              cache_fox_k, cache_fox_v, cache_fox_logf,
                              layer=j, n=n_s, t=t_s, nh=b_heads, qcol=qcol, pp=8)

            xp = out_proj(xp, ya_p, ob_p, w_out, tm=tm_p, tn=512)
            xs = out_proj(xs, ya_s, ob_s, w_out, tm=tm_s, tn=512)

            k_p = zp[:, 2 * a_w + b_w:2 * a_w + 2 * b_w]
            v_p = zp[:, 2 * a_w + 2 * b_w:2 * a_w + 3 * b_w]
            st_p['fox_k'].append(k_p.reshape(n_p, s_len, b_heads, HD))
            st_p['fox_v'].append(v_p.reshape(n_p, s_len, b_heads, HD))
            st_p['fox_logf'].append(lf_p.reshape(n_p, s_len, b_heads))
            st_p['lru_conv'].append(cv_p)
            st_p['lru_h'].append(h_p.reshape(n_p, a_w))
            st_s['fox_k'].append(k_s.reshape(n_s, t_s, b_heads, HD))
            st_s['fox_v'].append(v_s.reshape(n_s, t_s, b_heads, HD))
            st_s['fox_logf'].append(lf_s.reshape(n_s, t_s, b_heads))
            st_s['lru_conv'].append(cv_s)
            st_s['lru_h'].append(h_s.reshape(n_s, a_w))
        else:
            w_in = odd_w_in[j]
            w_out = odd_w_out[j]
            qcol = 4 * c_w // d_w
            zp = norm_matmul(xp, g_mix, w_in, tm=tm_p, tn=512)
            zs = norm_matmul(xs, g_mix, w_in, tm=tm_s, tn=512)

            oc_p, s_p = hgrn_mixer(zp, jnp.zeros((n_p, c_heads, HD, HD), F32), hgrn_lb_logits, hgrn_norm[j],
                                   n=n_p, t=s_len, layer=l)
            od_p = dsw_prompt(zp, n=n_p, s_len=s_len, nkv=d_heads, qcol=qcol)
            oc_s, s_s = hgrn_sample(zs, state_hgrn, hgrn_lb_logits, hgrn_norm[j],
                                    s_layer=j, n=n_s, t=t_s, layer=l)
            od_s = dsw_sample(zs, cache_dsw_k, cache_dsw_v,
                              layer=j, n=n_s, t=t_s, nkv=d_heads, qcol=qcol, row0=0)

            xp = out_proj(xp, oc_p, od_p, w_out, tm=tm_p, tn=512)
            xs = out_proj(xs, oc_s, od_s, w_out, tm=tm_s, tn=512)

            k0 = 4 * c_w + len(D_GROUPS) * d_w
            zp3 = zp.reshape(n_p, s_len, -1)
            st_p['hgrn'].append(s_p)
            st_p['dsw_k'].append(zp3[:, s_len - keep:, k0:k0 + d_w].reshape(n_p, keep, d_heads, HD))
            st_p['dsw_v'].append(zp3[:, s_len - keep:, k0 + d_w:k0 + 2 * d_w].reshape(n_p, keep, d_heads, HD))
            st_s['hgrn'].append(s_s)
            st_s['dsw_k'].append(zs[:, k0:k0 + d_w].reshape(n_s, t_s, d_heads, HD))
            st_s['dsw_v'].append(zs[:, k0 + d_w:k0 + 2 * d_w].reshape(n_s, t_s, d_heads, HD))
        xp, xs = ffn_both(xp, xs, l, 1, l == depth - 1)

    y_prompt = xp.reshape(n_p, s_len, d)
    y_sample = xs.reshape(n_s, t_s, d)
    return (y_prompt, y_sample,
            *(jnp.stack(st_p[nm]) for nm in names),
            *(jnp.stack(st_s[nm]) for nm in names))
```
